```python
import math
import jax, jax.numpy as jnp
from jax import lax
import numpy as np

D_MODEL = 2048
BATCH = 4
SEQ = 4096
DEPTH = 2

N_A = (DEPTH + 1) // 2
N_B = DEPTH - N_A

D_RNN = D_MODEL
LRU_BLOCK = 256
N_BLK = D_RNN // LRU_BLOCK
CONV_W = 4
LRU_C = 8.0

N_HEADS = D_MODEL // 128
NOPE_D = 128
ROPE_D = 64
QK_D = NOPE_D + ROPE_D
V_D = 128
Q_LORA = 512
KV_LORA = 512
ROPE_THETA = 10000.0
Q_BLOCK = 128

D_FF = ((8 * D_MODEL // 3 + 255) // 256) * 256

NORM_EPS = 1e-6

kernel_name = "yoco_rglru_mla_hybrid"


def rmsnorm(x, g):
    xf = x.astype(jnp.float32)
    y = xf * lax.rsqrt(jnp.mean(xf * xf, axis=-1, keepdims=True) + NORM_EPS)
    return (y * g.astype(jnp.float32)).astype(x.dtype)


def rope_tables(positions):
    inv_freq = ROPE_THETA ** (-jnp.arange(0, ROPE_D, 2, dtype=jnp.float32) / ROPE_D)
    ang = positions.astype(jnp.float32)[..., None] * inv_freq
    return jnp.cos(ang)[:, :, None, :], jnp.sin(ang)[:, :, None, :]


def apply_rope(t, cos, sin):
    cos = cos.astype(t.dtype)
    sin = sin.astype(t.dtype)
    t1, t2 = t[..., : ROPE_D // 2], t[..., ROPE_D // 2:]
    return jnp.concatenate([t1 * cos - t2 * sin, t2 * cos + t1 * sin], axis=-1)


def qk_finish(t, g, cos, sin):
    t = rmsnorm(t, g)
    return jnp.concatenate([t[..., :NOPE_D], apply_rope(t[..., NOPE_D:], cos, sin)], axis=-1)


def causal_conv(x, w, b):
    y = lax.conv_general_dilated(
        x, w[:, None, :].astype(x.dtype), window_strides=(1,), padding=[(CONV_W - 1, 0)],
        dimension_numbers=("NWC", "WIO", "NWC"), feature_group_count=x.shape[-1])
    return y + b


def rglru(x, w_gate, b_gate, lam):
    bsz, seq, _ = x.shape
    xb = x.reshape(bsz, seq, N_BLK, LRU_BLOCK)
    g = jnp.einsum("bsnd,nde->bsne", xb, w_gate) + b_gate
    g = jax.nn.sigmoid(g.astype(jnp.float32))
    r = g[..., :LRU_BLOCK].reshape(bsz, seq, D_RNN)
    i = g[..., LRU_BLOCK:].reshape(bsz, seq, D_RNN)
    log_a = -LRU_C * r * jax.nn.softplus(-lam.astype(jnp.float32))
    a = jnp.exp(log_a)
    mult = jnp.sqrt(-jnp.expm1(2.0 * log_a))
    bterm = mult * (i * x.astype(jnp.float32))

    def combine(left, right):
        a1, b1 = left
        a2, b2 = right
        return a1 * a2, a2 * b1 + b2

    _, h = lax.associative_scan(combine, (a, bterm), axis=1)
    return h.astype(x.dtype)


def rglru_block(hn, w_in, b_in, conv_w, conv_b, w_gate, b_gate, lam, w_out, b_out):
    u = hn @ w_in + b_in
    gate_br = u[..., :D_RNN]
    rec = causal_conv(u[..., D_RNN:], conv_w, conv_b)
    rec = rglru(rec, w_gate, b_gate, lam)
    return (jax.nn.gelu(gate_br) * rec) @ w_out + b_out


def shared_kv(h, kv_norm_in, w_dkv, kv_latent_norm, w_ukv, k_norm, cos, sin):
    bsz, seq, _ = h.shape
    hn = rmsnorm(h, kv_norm_in)
    ckr = hn @ w_dkv
    c_kv = rmsnorm(ckr[..., :KV_LORA], kv_latent_norm)
    k_rope = ckr[..., KV_LORA:]
    kv = jnp.einsum("bsc,chd->bshd", c_kv, w_ukv)
    k_nope, v = kv[..., :NOPE_D], kv[..., NOPE_D:]
    k = jnp.concatenate(
        [k_nope, jnp.broadcast_to(k_rope[:, :, None, :], (bsz, seq, N_HEADS, ROPE_D))], axis=-1)
    k = qk_finish(k, k_norm, cos, sin)
    return k.transpose(0, 2, 1, 3), v.transpose(0, 2, 1, 3)


def causal_attention(q, k, v):
    bsz, nh, seq, dk = q.shape
    n_blk = seq // Q_BLOCK
    scale = 1.0 / math.sqrt(QK_D)
    qb = q.reshape(bsz, nh, n_blk, Q_BLOCK, dk).transpose(2, 0, 1, 3, 4)
    key_idx = jnp.arange(seq)

    def one_block(args):
        q_blk, bi = args
        s = jnp.einsum("bhqd,bhkd->bhqk", q_blk, k).astype(jnp.float32) * scale
        q_idx = bi * Q_BLOCK + jnp.arange(Q_BLOCK)
        mask = key_idx[None, :] <= q_idx[:, None]
        p = jax.nn.softmax(jnp.where(mask, s, -jnp.inf), axis=-1)
        return jnp.einsum("bhqk,bhkd->bhqd", p.astype(v.dtype), v)

    out = lax.map(one_block, (qb, jnp.arange(n_blk)))
    return out.transpose(1, 0, 3, 2, 4).reshape(bsz, seq, nh, v.shape[-1])


def mla_block(hn, k_sh, v_sh, cos, sin, w_dq, q_latent_norm, w_uq, q_norm, w_o):
    bsz, seq, _ = hn.shape
    cq = rmsnorm(hn @ w_dq, q_latent_norm)
    q = jnp.einsum("bsc,chd->bshd", cq, w_uq)
    q = qk_finish(q, q_norm, cos, sin).transpose(0, 2, 1, 3)
    o = causal_attention(q, k_sh, v_sh)
    return o.reshape(bsz, seq, N_HEADS * V_D) @ w_o


def swiglu(hn, w_in, w_out):
    u = hn @ w_in
    return (jax.nn.silu(u[..., :D_FF]) * u[..., D_FF:]) @ w_out


def setup_inputs(seed: int = 0) -> dict:
    key = jax.random.key(seed)
    ks = jax.random.split(key, 32)
    f32 = jnp.float32

    def w(k, shape, fan_in):
        return jax.random.normal(k, shape, f32) * (fan_in ** -0.5)

    def gain(k, shape):
        return 1.0 + 0.02 * jax.random.normal(k, shape, f32)

    def bias(k, shape):
        return 0.02 * jax.random.normal(k, shape, f32)

    x = jax.random.normal(ks[0], (BATCH, SEQ, D_MODEL), f32)
    offset = jax.random.randint(ks[1], (BATCH, 1), 0, 1024, dtype=jnp.int32)
    positions = offset + jnp.arange(SEQ, dtype=jnp.int32)[None, :]

    u = jax.random.uniform(ks[2], (N_A, D_RNN), f32, minval=0.9, maxval=0.999)
    a0 = u ** (1.0 / LRU_C)
    lru_lambda = jnp.log(a0) - jnp.log1p(-a0)

    return {
        "x": x,
        "positions": positions,
        "norm_mix": gain(ks[3], (DEPTH, D_MODEL)),
        "norm_ffn": gain(ks[4], (DEPTH, D_MODEL)),
        "lru_w_in": w(ks[5], (N_A, D_MODEL, 2 * D_RNN), D_MODEL),
        "lru_b_in": bias(ks[6], (N_A, 2 * D_RNN)),
        "lru_conv_w": w(ks[7], (N_A, CONV_W, D_RNN), CONV_W),
        "lru_conv_b": bias(ks[8], (N_A, D_RNN)),
        "lru_w_gate": w(ks[9], (N_A, N_BLK, LRU_BLOCK, 2 * LRU_BLOCK), LRU_BLOCK),
        "lru_b_gate": bias(ks[10], (N_A, N_BLK, 2 * LRU_BLOCK)),
        "lru_lambda": lru_lambda,
        "lru_w_out": w(ks[11], (N_A, D_RNN, D_MODEL), D_RNN),
        "lru_b_out": bias(ks[12], (N_A, D_MODEL)),
        "kv_norm_in": gain(ks[13], (D_MODEL,)),
        "w_dkv": w(ks[14], (D_MODEL, KV_LORA + ROPE_D), D_MODEL),
        "kv_latent_norm": gain(ks[15], (KV_LORA,)),
        "w_ukv": w(ks[16], (KV_LORA, N_HEADS, NOPE_D + V_D), KV_LORA),
        "k_norm": gain(ks[17], (QK_D,)),
        "w_dq": w(ks[18], (N_B, D_MODEL, Q_LORA), D_MODEL),
        "q_latent_norm": gain(ks[19], (N_B, Q_LORA)),
        "w_uq": w(ks[20], (N_B, Q_LORA, N_HEADS, QK_D), Q_LORA),
        "q_norm": gain(ks[21], (N_B, QK_D)),
        "w_o": w(ks[22], (N_B, N_HEADS * V_D, D_MODEL), N_HEADS * V_D),
        "ffn_w_in": w(ks[23], (DEPTH, D_MODEL, 2 * D_FF), D_MODEL),
        "ffn_w_out": w(ks[24], (DEPTH, D_FF, D_MODEL), D_FF),
    }


def reference(x, positions, norm_mix, norm_ffn, lru_w_in, lru_b_in, lru_conv_w, lru_conv_b,
              lru_w_gate, lru_b_gate, lru_lambda, lru_w_out, lru_b_out, kv_norm_in, w_dkv,
              kv_latent_norm, w_ukv, k_norm, w_dq, q_latent_norm, w_uq, q_norm, w_o,
              ffn_w_in, ffn_w_out):
    cos, sin = rope_tables(positions)
    h = x
    k_sh = None
    v_sh = None
    for layer in range(DEPTH):
        hn = rmsnorm(h, norm_mix[layer])
        if layer < N_A:
            h = h + rglru_block(hn, lru_w_in[layer], lru_b_in[layer], lru_conv_w[layer],
                                lru_conv_b[layer], lru_w_gate[layer], lru_b_gate[layer],
                                lru_lambda[layer], lru_w_out[layer], lru_b_out[layer])
        else:
            j = layer - N_A
            h = h + mla_block(hn, k_sh, v_sh, cos, sin, w_dq[j], q_latent_norm[j], w_uq[j],
                              q_norm[j], w_o[j])
        h = h + swiglu(rmsnorm(h, norm_ffn[layer]), ffn_w_in[layer], ffn_w_out[layer])
        if layer == N_A - 1:
            k_sh, v_sh = shared_kv(h, kv_norm_in, w_dkv, kv_latent_norm, w_ukv, k_norm, cos, sin)
    return h
```

```python
import functools
import math

import jax
import jax.numpy as jnp
from jax import lax
from jax.experimental import pallas as pl
from jax.experimental.pallas import tpu as pltpu

F32 = jnp.float32
BF16 = jnp.bfloat16

D_MODEL = 2048
BATCH = 4
SEQ = 4096
TOKENS = BATCH * SEQ
D_RNN = D_MODEL
LRU_BLOCK = 256
N_BLK = D_RNN // LRU_BLOCK
CONV_W = 4
LRU_C = 8.0
N_HEADS = 16
NOPE_D = 128
ROPE_D = 64
QK_D = NOPE_D + ROPE_D
V_D = 128
Q_LORA = 512
KV_LORA = 512
ROPE_THETA = 10000.0
D_FF = 5632
NORM_EPS = 1e-6

LANES = 128
SUBLANES = 8
HEAD_PAD = 2 * LANES
LANE_GROUPS = D_RNN // LANES
SCAN_GROUPS = 8
V7X_VMEM_BYTES = 64 * 1024 * 1024
VMEM_LIMIT = V7X_VMEM_BYTES * 7 // 8
NEG_BIG = -1e30


def _params(*semantics):
    return pltpu.CompilerParams(dimension_semantics=semantics, vmem_limit_bytes=VMEM_LIMIT)


def _rms(x, g):
    ms = jnp.mean(x * x, axis=-1, keepdims=True)
    return x * lax.rsqrt(ms + NORM_EPS) * g


def _rope(t, c, s1, s2):
    return t * c + pltpu.roll(t, LANES - ROPE_D // 2, 1) * s1 + pltpu.roll(t, ROPE_D // 2, 1) * s2


def _rope_table_kernel(pos_ref, invf_ref, cos_ref, sin_ref):
    ang = pos_ref[...].astype(F32) * invf_ref[...]
    cos_ref[...] = jnp.cos(ang)
    sin_ref[...] = jnp.sin(ang)


def _rope_tables(positions):
    half = ROPE_D // 2
    per_row = LANES // half
    rows = TOKENS // per_row
    inv_freq = ROPE_THETA ** (-jnp.arange(0, ROPE_D, 2, dtype=F32) / ROPE_D)
    pos = jnp.repeat(positions.reshape(rows, per_row), half, axis=1)
    invf = jnp.tile(inv_freq, per_row).reshape(1, LANES)
    tr = 512
    cosp, sinp = pl.pallas_call(
        _rope_table_kernel,
        grid=(rows // tr,),
        in_specs=[pl.BlockSpec((tr, LANES), lambda i: (i, 0)),
                  pl.BlockSpec((1, LANES), lambda i: (0, 0))],
        out_specs=[pl.BlockSpec((tr, LANES), lambda i: (i, 0)),
                   pl.BlockSpec((tr, LANES), lambda i: (i, 0))],
        out_shape=[jax.ShapeDtypeStruct((rows, LANES), F32)] * 2,
        compiler_params=_params("arbitrary"),
        name="rope_tables",
    )(pos, invf)
    cos = cosp.reshape(TOKENS, half)
    sin = sinp.reshape(TOKENS, half)
    z1 = jnp.zeros((TOKENS, half), F32)
    z2 = jnp.zeros((TOKENS, 2 * half), F32)
    rc = jnp.concatenate([cos, cos, z2], axis=1)
    rs1 = jnp.concatenate([-sin, z1, z2], axis=1)
    rs2 = jnp.concatenate([z1, sin, z2], axis=1)
    return rc, rs1, rs2


def _linear_kernel(*refs, has_gain, has_bias, has_resid, stage_x):
    refs = list(refs)
    x_ref = refs.pop(0)
    w_ref = refs.pop(0)
    g_ref = refs.pop(0) if has_gain else None
    b_ref = refs.pop(0) if has_bias else None
    r_ref = refs.pop(0) if has_resid else None
    o_ref = refs.pop(0)
    if stage_x:
        xs_ref = refs.pop(0)

        @pl.when(pl.program_id(1) == 0)
        def _():
            x = x_ref[...].astype(F32)
            if has_gain:
                x = _rms(x, g_ref[...])
            xs_ref[...] = x.astype(BF16)

        xv = xs_ref[...]
    else:
        xv = x_ref[...]
    acc = jnp.dot(xv, w_ref[...], preferred_element_type=F32)
    if has_bias:
        acc = acc + b_ref[...]
    if has_resid:
        acc = acc + r_ref[...]
    o_ref[...] = acc.astype(o_ref.dtype)


def _linear(x, w, *, gain=None, bias=None, resid=None, out_dtype=F32, tm=512, tn=1024, name):
    m, k = x.shape
    n = w.shape[1]
    tn = min(tn, n)
    stage_x = gain is not None or x.dtype != BF16
    in_specs = [pl.BlockSpec((tm, k), lambda i, j: (i, 0)),
                pl.BlockSpec((k, tn), lambda i, j: (0, j))]
    args = [x, w]
    if gain is not None:
        in_specs.append(pl.BlockSpec((1, k), lambda i, j: (0, 0)))
        args.append(gain.reshape(1, k))
    if bias is not None:
        in_specs.append(pl.BlockSpec((1, tn), lambda i, j: (0, j)))
        args.append(bias.reshape(1, n))
    if resid is not None:
        in_specs.append(pl.BlockSpec((tm, tn), lambda i, j: (i, j)))
        args.append(resid)
    scratch = [pltpu.VMEM((tm, k), BF16)] if stage_x else []
    return pl.pallas_call(
        functools.partial(_linear_kernel, has_gain=gain is not None, has_bias=bias is not None,
                          has_resid=resid is not None, stage_x=stage_x),
        grid=(m // tm, n // tn),
        in_specs=in_specs,
        out_specs=pl.BlockSpec((tm, tn), lambda i, j: (i, j)),
        out_shape=jax.ShapeDtypeStruct((m, n), out_dtype),
        scratch_shapes=scratch,
        compiler_params=_params("parallel", "arbitrary"),
        name=name,
    )(*args)


def _inproj_kernel(x_ref, g_ref, wa_ref, wb_ref, ba_ref, bb_ref, gate_ref, rec_ref, xs_ref):
    @pl.when(pl.program_id(1) == 0)
    def _():
        xs_ref[...] = _rms(x_ref[...], g_ref[...]).astype(BF16)

    xv = xs_ref[...]
    ga = jnp.dot(xv, wa_ref[...], preferred_element_type=F32) + ba_ref[...]
    gate_ref[...] = jax.nn.gelu(ga).astype(gate_ref.dtype)
    rec_ref[...] = jnp.dot(xv, wb_ref[...], preferred_element_type=F32) + bb_ref[...]


def _inproj(x, gain, w_in, b_in, *, tm=512, tn=512):
    nj = D_RNN // tn
    b2 = b_in.reshape(1, 2 * D_RNN)
    return pl.pallas_call(
        _inproj_kernel,
        grid=(TOKENS // tm, nj),
        in_specs=[pl.BlockSpec((tm, D_MODEL), lambda i, j: (i, 0)),
                  pl.BlockSpec((1, D_MODEL), lambda i, j: (0, 0)),
                  pl.BlockSpec((D_MODEL, tn), lambda i, j: (0, j)),
                  pl.BlockSpec((D_MODEL, tn), lambda i, j: (0, j + nj)),
                  pl.BlockSpec((1, tn), lambda i, j: (0, j)),
                  pl.BlockSpec((1, tn), lambda i, j: (0, j + nj))],
        out_specs=[pl.BlockSpec((tm, tn), lambda i, j: (i, j)),
                   pl.BlockSpec((tm, tn), lambda i, j: (i, j))],
        out_shape=[jax.ShapeDtypeStruct((TOKENS, D_RNN), BF16),
                   jax.ShapeDtypeStruct((TOKENS, D_RNN), F32)],
        scratch_shapes=[pltpu.VMEM((tm, D_MODEL), BF16)],
        compiler_params=_params("parallel", "arbitrary"),
        name="lru_in_proj",
    )(x, gain.reshape(1, D_MODEL), w_in, w_in, b2, b2)


def _rglru_kernel(x_ref, gate_ref, cw_ref, cb_ref, wg_ref, bg_ref, lam_ref, y_ref,
                  xpad_s, a_s, b_s, eq_s, cm_s, hc_s, *, chunk):
    c = pl.program_id(1)
    seg = chunk // SUBLANES

    @pl.when(c == 0)
    def _():
        xpad_s[0:SUBLANES, :] = jnp.zeros((SUBLANES, D_RNN), F32)
        hc_s[...] = jnp.zeros_like(hc_s)

    xpad_s[SUBLANES:SUBLANES + chunk, :] = x_ref[...]
    rec = cb_ref[...]
    for k in range(CONV_W):
        off = SUBLANES - (CONV_W - 1) + k
        rec = rec + cw_ref[k:k + 1, :] * xpad_s[off:off + chunk, :]
    xpad_s[0:SUBLANES, :] = xpad_s[chunk:chunk + SUBLANES, :]

    lam = lam_ref[...]
    sp = jnp.maximum(-lam, 0.0) + jnp.log1p(jnp.exp(-jnp.abs(lam)))
    for n in range(N_BLK):
        cols = slice(n * LRU_BLOCK, (n + 1) * LRU_BLOCK)
        xn = rec[:, cols]
        g = jnp.dot(xn.astype(BF16), wg_ref[n], preferred_element_type=F32) + bg_ref[n:n + 1, :]
        r = jax.nn.sigmoid(g[:, :LRU_BLOCK])
        i = jax.nn.sigmoid(g[:, LRU_BLOCK:])
        log_a = (-LRU_C) * r * sp[:, cols]
        a = jnp.exp(log_a)
        mult = jnp.sqrt(-jnp.tanh(log_a) * (a * a + 1.0))
        b = mult * (i * xn)
        for sub in range(LRU_BLOCK // LANES):
            lanes = slice(sub * LANES, (sub + 1) * LANES)
            a_s[n * (LRU_BLOCK // LANES) + sub] = a[:, lanes]
            b_s[n * (LRU_BLOCK // LANES) + sub] = b[:, lanes]

    for g0 in range(0, LANE_GROUPS, SCAN_GROUPS):
        groups = range(g0, g0 + SCAN_GROUPS)

        def rows(ref, g, s):
            return ref[g, pl.ds(s, SUBLANES, stride=seg), :]

        def pass1(s, carry):
            hs, ps = carry
            a = [rows(a_s, g, s) for g in groups]
            b = [rows(b_s, g, s) for g in groups]
            return (tuple(ai * hi + bi for ai, hi, bi in zip(a, hs, b)),
                    tuple(ai * pi for ai, pi in zip(a, ps)))

        zero = jnp.zeros((SUBLANES, LANES), F32)
        one = jnp.ones((SUBLANES, LANES), F32)
        e, q = lax.fori_loop(0, seg, pass1, ((zero,) * SCAN_GROUPS, (one,) * SCAN_GROUPS),
                             unroll=4)
        for idx, g in enumerate(groups):
            eq_s[0] = e[idx]
            eq_s[1] = q[idx]
            h = hc_s[g, 0:1, :]
            for j in range(SUBLANES):
                cm_s[g, j:j + 1, :] = h
                h = eq_s[1, j:j + 1, :] * h + eq_s[0, j:j + 1, :]
            hc_s[g, 0:1, :] = h

        def pass2(s, hs):
            out = []
            for g, h in zip(groups, hs):
                h = rows(a_s, g, s) * h + rows(b_s, g, s)
                b_s[g, pl.ds(s, SUBLANES, stride=seg), :] = h
                out.append(h)
            return tuple(out)

        lax.fori_loop(0, seg, pass2, tuple(cm_s[g] for g in groups), unroll=4)

    for g in range(LANE_GROUPS):
        cols = slice(g * LANES, (g + 1) * LANES)
        y_ref[:, cols] = (gate_ref[:, cols].astype(F32) * b_s[g]).astype(y_ref.dtype)


def _rglru(rec_pre, gate, conv_w, conv_b, w_gate, b_gate, lam, *, chunk=512):
    nc = SEQ // chunk
    row = lambda b, c: (b * nc + c, 0)
    full2 = lambda b, c: (0, 0)
    return pl.pallas_call(
        functools.partial(_rglru_kernel, chunk=chunk),
        grid=(BATCH, nc),
        in_specs=[pl.BlockSpec((chunk, D_RNN), row),
                  pl.BlockSpec((chunk, D_RNN), row),
                  pl.BlockSpec((CONV_W, D_RNN), full2),
                  pl.BlockSpec((1, D_RNN), full2),
                  pl.BlockSpec((N_BLK, LRU_BLOCK, 2 * LRU_BLOCK), lambda b, c: (0, 0, 0)),
                  pl.BlockSpec((N_BLK, 2 * LRU_BLOCK), full2),
                  pl.BlockSpec((1, D_RNN), full2)],
        out_specs=pl.BlockSpec((chunk, D_RNN), row),
        out_shape=jax.ShapeDtypeStruct((TOKENS, D_RNN), BF16),
        scratch_shapes=[pltpu.VMEM((chunk + SUBLANES, D_RNN), F32),
                        pltpu.VMEM((LANE_GROUPS, chunk, LANES), F32),
                        pltpu.VMEM((LANE_GROUPS, chunk, LANES), F32),
                        pltpu.VMEM((2, SUBLANES, LANES), F32),
                        pltpu.VMEM((LANE_GROUPS, SUBLANES, LANES), F32),
                        pltpu.VMEM((LANE_GROUPS, SUBLANES, LANES), F32)],
        compiler_params=_params("parallel", "arbitrary"),
        name="rglru_core",
    )(rec_pre, gate, conv_w, conv_b.reshape(1, D_RNN), w_gate, b_gate, lam.reshape(1, D_RNN))


def _ffn_kernel(h_ref, g_ref, wa_ref, wb_ref, w2_ref, o_ref, hn_s):
    f = pl.program_id(1)

    @pl.when(f == 0)
    def _():
        h = h_ref[...]
        hn_s[...] = _rms(h, g_ref[...]).astype(BF16)
        o_ref[...] = h

    hn = hn_s[...]
    ua = jnp.dot(hn, wa_ref[...], preferred_element_type=F32)
    ub = jnp.dot(hn, wb_ref[...], preferred_element_type=F32)
    act = (jax.nn.silu(ua) * ub).astype(BF16)
    o_ref[...] += jnp.dot(act, w2_ref[...], preferred_element_type=F32)


def _ffn(h, gain, w_in, w_out, *, tm=512, tf=512):
    nf = D_FF // tf
    return pl.pallas_call(
        _ffn_kernel,
        grid=(TOKENS // tm, nf),
        in_specs=[pl.BlockSpec((tm, D_MODEL), lambda i, f: (i, 0)),
                  pl.BlockSpec((1, D_MODEL), lambda i, f: (0, 0)),
                  pl.BlockSpec((D_MODEL, tf), lambda i, f: (0, f)),
                  pl.BlockSpec((D_MODEL, tf), lambda i, f: (0, f + nf)),
                  pl.BlockSpec((tf, D_MODEL), lambda i, f: (f, 0))],
        out_specs=pl.BlockSpec((tm, D_MODEL), lambda i, f: (i, 0)),
        out_shape=jax.ShapeDtypeStruct((TOKENS, D_MODEL), F32),
        scratch_shapes=[pltpu.VMEM((tm, D_MODEL), BF16)],
        compiler_params=_params("parallel", "arbitrary"),
        name="swiglu_ffn",
    )(h, gain.reshape(1, D_MODEL), w_in, w_in, w_out)


def _kv_heads_kernel(c_ref, kr_ref, gl_ref, w_ref, gn_ref, gr_ref, rc_ref, rs1_ref, rs2_ref,
                     k_ref, v_ref, cn_s, krr_s, ssr_s):
    @pl.when(pl.program_id(1) == 0)
    def _():
        cn_s[...] = _rms(c_ref[...], gl_ref[...]).astype(BF16)
        kr = kr_ref[...]
        ssr_s[...] = jnp.broadcast_to(jnp.sum(kr * kr, axis=1, keepdims=True), ssr_s.shape)
        krr_s[...] = _rope(kr * gr_ref[...], rc_ref[...], rs1_ref[...], rs2_ref[...])

    kv = jnp.dot(cn_s[...], w_ref[...], preferred_element_type=F32)
    kn = kv[:, :NOPE_D]
    ss = jnp.sum(kn * kn, axis=1, keepdims=True) + ssr_s[:, 0:1]
    rstd = lax.rsqrt(ss * (1.0 / QK_D) + NORM_EPS)
    k_ref[:, :NOPE_D] = (kn * rstd * gn_ref[...]).astype(k_ref.dtype)
    k_ref[:, NOPE_D:] = (krr_s[...] * rstd).astype(k_ref.dtype)
    v_ref[...] = kv[:, NOPE_D:].astype(v_ref.dtype)


def _kv_heads(ckr, gl, w_ukv, gn, gr, rc, rs1, rs2, *, tm=1024):
    rope_blk = KV_LORA // LANES
    row = lambda i, h: (i, 0)
    vec = lambda i, h: (0, 0)
    return pl.pallas_call(
        _kv_heads_kernel,
        grid=(TOKENS // tm, N_HEADS),
        in_specs=[pl.BlockSpec((tm, KV_LORA), row),
                  pl.BlockSpec((tm, LANES), lambda i, h: (i, rope_blk)),
                  pl.BlockSpec((1, KV_LORA), vec),
                  pl.BlockSpec((KV_LORA, NOPE_D + V_D), lambda i, h: (0, h)),
                  pl.BlockSpec((1, LANES), vec),
                  pl.BlockSpec((1, LANES), vec),
                  pl.BlockSpec((tm, LANES), row),
                  pl.BlockSpec((tm, LANES), row),
                  pl.BlockSpec((tm, LANES), row)],
        out_specs=[pl.BlockSpec((tm, HEAD_PAD), lambda i, h: (i, h)),
                   pl.BlockSpec((tm, V_D), lambda i, h: (i, h))],
        out_shape=[jax.ShapeDtypeStruct((TOKENS, N_HEADS * HEAD_PAD), BF16),
                   jax.ShapeDtypeStruct((TOKENS, N_HEADS * V_D), BF16)],
        scratch_shapes=[pltpu.VMEM((tm, KV_LORA), BF16),
                        pltpu.VMEM((tm, LANES), F32),
                        pltpu.VMEM((tm, LANES), F32)],
        compiler_params=_params("parallel", "arbitrary"),
        name="kv_heads",
    )(ckr, ckr, gl.reshape(1, KV_LORA), w_ukv, gn, gr, rc, rs1, rs2)


def _q_heads_kernel(c_ref, gl_ref, w_ref, gn_ref, gr_ref, rc_ref, rs1_ref, rs2_ref,
                    q_ref, cn_s):
    @pl.when(pl.program_id(1) == 0)
    def _():
        cn_s[...] = _rms(c_ref[...], gl_ref[...]).astype(BF16)

    q = jnp.dot(cn_s[...], w_ref[...], preferred_element_type=F32)
    qn = q[:, :NOPE_D]
    qr = q[:, NOPE_D:]
    ss = jnp.sum(qn * qn, axis=1, keepdims=True) + jnp.sum(qr * qr, axis=1, keepdims=True)
    rstd = lax.rsqrt(ss * (1.0 / QK_D) + NORM_EPS)
    q_ref[:, :NOPE_D] = (qn * rstd * gn_ref[...]).astype(q_ref.dtype)
    roped = _rope(qr * gr_ref[...], rc_ref[...], rs1_ref[...], rs2_ref[...])
    q_ref[:, NOPE_D:] = (roped * rstd).astype(q_ref.dtype)


def _q_heads(cq, gl, w_uq, gn, gr, rc, rs1, rs2, *, tm=1024):
    row = lambda i, h: (i, 0)
    vec = lambda i, h: (0, 0)
    return pl.pallas_call(
        _q_heads_kernel,
        grid=(TOKENS // tm, N_HEADS),
        in_specs=[pl.BlockSpec((tm, Q_LORA), row),
                  pl.BlockSpec((1, Q_LORA), vec),
                  pl.BlockSpec((Q_LORA, HEAD_PAD), lambda i, h: (0, h)),
                  pl.BlockSpec((1, LANES), vec),
                  pl.BlockSpec((1, LANES), vec),
                  pl.BlockSpec((tm, LANES), row),
                  pl.BlockSpec((tm, LANES), row),
                  pl.BlockSpec((tm, LANES), row)],
        out_specs=pl.BlockSpec((tm, HEAD_PAD), lambda i, h: (i, h)),
        out_shape=jax.ShapeDtypeStruct((TOKENS, N_HEADS * HEAD_PAD), BF16),
        scratch_shapes=[pltpu.VMEM((tm, Q_LORA), BF16)],
        compiler_params=_params("parallel", "arbitrary"),
        name="q_heads",
    )(cq, gl.reshape(1, Q_LORA), w_uq, gn, gr, rc, rs1, rs2)


def _attn_kernel(q_ref, k_ref, v_ref, o_ref, m_s, l_s, acc_s, *, blk):
    i = pl.program_id(2)
    q = q_ref[...]
    m_s[...] = jnp.full_like(m_s, NEG_BIG)
    l_s[...] = jnp.zeros_like(l_s)
    acc_s[...] = jnp.zeros_like(acc_s)

    def step(j, masked):
        start = pl.multiple_of(j * blk, blk)
        k = k_ref[pl.ds(start, blk), :]
        v = v_ref[pl.ds(start, blk), :]
        s = lax.dot_general(q, k, (((1,), (1,)), ((), ())), preferred_element_type=F32)
        if masked:
            row = lax.broadcasted_iota(jnp.int32, (blk, blk), 0)
            col = lax.broadcasted_iota(jnp.int32, (blk, blk), 1)
            s = jnp.where(col <= row, s, NEG_BIG)
        m_prev = m_s[...]
        m_new = jnp.maximum(m_prev, jnp.max(s, axis=1, keepdims=True))
        p = jnp.exp2(s - m_new)
        alpha = jnp.exp2(m_prev - m_new)
        l_s[...] = alpha * l_s[...] + jnp.sum(p, axis=1, keepdims=True)
        acc_s[...] = alpha * acc_s[...] + jnp.dot(p.astype(BF16), v, preferred_element_type=F32)
        m_s[...] = m_new

    def body(j, carry):
        step(j, False)
        return carry

    lax.fori_loop(0, i, body, 0)
    step(i, True)
    o_ref[...] = (acc_s[...] / l_s[...]).astype(o_ref.dtype)


def _attention(q, k, v, *, blk=512):
    nq = SEQ // blk
    return pl.pallas_call(
        functools.partial(_attn_kernel, blk=blk),
        grid=(BATCH, N_HEADS, nq),
        in_specs=[pl.BlockSpec((blk, HEAD_PAD), lambda b, h, i: (b * nq + i, h)),
                  pl.BlockSpec((SEQ, HEAD_PAD), lambda b, h, i: (b, h)),
                  pl.BlockSpec((SEQ, V_D), lambda b, h, i: (b, h))],
        out_specs=pl.BlockSpec((blk, V_D), lambda b, h, i: (b * nq + i, h)),
        out_shape=jax.ShapeDtypeStruct((TOKENS, N_HEADS * V_D), BF16),
        scratch_shapes=[pltpu.VMEM((blk, 1), F32),
                        pltpu.VMEM((blk, 1), F32),
                        pltpu.VMEM((blk, V_D), F32)],
        compiler_params=_params("parallel", "parallel", "arbitrary"),
        name="flash_attention",
    )(q, k, v)


def _split_qk_gain(g, scale):
    gn = (g[:NOPE_D] * scale).reshape(1, LANES)
    gr = jnp.pad(g[NOPE_D:] * scale, (0, LANES - ROPE_D)).reshape(1, LANES)
    return gn, gr


def kernel(x, positions, norm_mix, norm_ffn, lru_w_in, lru_b_in, lru_conv_w, lru_conv_b, lru_w_gate, lru_b_gate, lru_lambda, lru_w_out, lru_b_out, kv_norm_in, w_dkv, kv_latent_norm, w_ukv, k_norm, w_dq, q_latent_norm, w_uq, q_norm, w_o, ffn_w_in, ffn_w_out):
    h = x.reshape(TOKENS, D_MODEL)
    rc, rs1, rs2 = _rope_tables(positions)

    gate, rec_pre = _inproj(h, norm_mix[0], lru_w_in[0].astype(BF16), lru_b_in[0])
    y = _rglru(rec_pre, gate, lru_conv_w[0], lru_conv_b[0], lru_w_gate[0].astype(BF16),
               lru_b_gate[0], lru_lambda[0])
    h = _linear(y, lru_w_out[0].astype(BF16), bias=lru_b_out[0], resid=h, name="lru_out_proj")
    h = _ffn(h, norm_ffn[0], ffn_w_in[0].astype(BF16), ffn_w_out[0].astype(BF16))

    w_dkv_p = jnp.pad(w_dkv, ((0, 0), (0, LANES - ROPE_D))).astype(BF16)
    ckr = _linear(h, w_dkv_p, gain=kv_norm_in, tn=KV_LORA + LANES, name="kv_down_proj")
    kgn, kgr = _split_qk_gain(k_norm, 1.0)
    k_sh, v_sh = _kv_heads(ckr, kv_latent_norm,
                           w_ukv.reshape(KV_LORA, N_HEADS * (NOPE_D + V_D)).astype(BF16),
                           kgn, kgr, rc, rs1, rs2)

    cq = _linear(h, w_dq[0].astype(BF16), gain=norm_mix[1], tn=Q_LORA, name="q_down_proj")
    w_uq_p = jnp.pad(w_uq[0], ((0, 0), (0, 0), (0, HEAD_PAD - QK_D)))
    qgn, qgr = _split_qk_gain(q_norm[0], math.log2(math.e) / math.sqrt(QK_D))
    q = _q_heads(cq, q_latent_norm[0], w_uq_p.reshape(Q_LORA, N_HEADS * HEAD_PAD).astype(BF16),
                 qgn, qgr, rc, rs1, rs2)
    o = _attention(q, k_sh, v_sh)
    h = _linear(o, w_o[0].astype(BF16), resid=h, name="attn_out_proj")
    h = _ffn(h, norm_ffn[1], ffn_w_in[1].astype(BF16), ffn_w_out[1].astype(BF16))
    return h.reshape(BATCH, SEQ, D_MODEL)
```

```python
import functools
import math

import jax
import jax.numpy as jnp
from jax import lax
from jax.experimental import pallas as pl
from jax.experimental.pallas import tpu as pltpu

F32 = jnp.float32
BF16 = jnp.bfloat16

D_MODEL = 2048
BATCH = 4
SEQ = 4096
TOKENS = BATCH * SEQ
D_RNN = D_MODEL
LRU_BLOCK = 256
N_BLK = D_RNN // LRU_BLOCK
CONV_W = 4
LRU_C = 8.0
N_HEADS = 16
NOPE_D = 128
ROPE_D = 64
QK_D = NOPE_D + ROPE_D
V_D = 128
Q_LORA = 512
KV_LORA = 512
ROPE_THETA = 10000.0
D_FF = 5632
NORM_EPS = 1e-6

LANES = 128
SUBLANES = 8
HEAD_PAD = 2 * LANES
Q_HEAD_COLS = 3 * LANES
LANE_GROUPS = D_RNN // LANES
SCAN_GROUPS = 8
V7X_VMEM_BYTES = 64 * 1024 * 1024
VMEM_LIMIT = V7X_VMEM_BYTES * 7 // 8
NEG_BIG = -1e30


def _params(*semantics):
    return pltpu.CompilerParams(dimension_semantics=semantics, vmem_limit_bytes=VMEM_LIMIT)


def _rms(x, g):
    ms = jnp.mean(x * x, axis=-1, keepdims=True)
    return x * lax.rsqrt(ms + NORM_EPS) * g


def _rope(t, c, s):
    swapped = pltpu.roll(t, LANES - ROPE_D // 2, 1) + pltpu.roll(t, ROPE_D // 2, 1)
    return t * c + swapped * s


def _rope_table_kernel(pos_ref, invf_ref, cos_ref, sin_ref):
    ang = pos_ref[...].astype(F32) * invf_ref[...]
    cos_ref[...] = jnp.cos(ang)
    sin_ref[...] = jnp.sin(ang)


def _rope_tables(positions):
    half = ROPE_D // 2
    per_row = LANES // half
    rows = TOKENS // per_row
    inv_freq = ROPE_THETA ** (-jnp.arange(0, ROPE_D, 2, dtype=F32) / ROPE_D)
    pos = jnp.repeat(positions.reshape(rows, per_row), half, axis=1)
    invf = jnp.tile(inv_freq, per_row).reshape(1, LANES)
    tr = 512
    cosp, sinp = pl.pallas_call(
        _rope_table_kernel,
        grid=(rows // tr,),
        in_specs=[pl.BlockSpec((tr, LANES), lambda i: (i, 0)),
                  pl.BlockSpec((1, LANES), lambda i: (0, 0))],
        out_specs=[pl.BlockSpec((tr, LANES), lambda i: (i, 0)),
                   pl.BlockSpec((tr, LANES), lambda i: (i, 0))],
        out_shape=[jax.ShapeDtypeStruct((rows, LANES), F32)] * 2,
        compiler_params=_params("arbitrary"),
        name="rope_tables",
    )(pos, invf)
    cos = cosp.reshape(TOKENS, half)
    sin = sinp.reshape(TOKENS, half)
    z = jnp.zeros((TOKENS, LANES - ROPE_D), F32)
    rc = jnp.concatenate([cos, cos, z], axis=1)
    rs = jnp.concatenate([-sin, sin, z], axis=1)
    return rc, rs


def _linear_kernel(*refs, has_gain, has_bias, has_resid, stage_x):
    refs = list(refs)
    x_ref = refs.pop(0)
    w_ref = refs.pop(0)
    g_ref = refs.pop(0) if has_gain else None
    b_ref = refs.pop(0) if has_bias else None
    r_ref = refs.pop(0) if has_resid else None
    o_ref = refs.pop(0)
    if stage_x:
        xs_ref = refs.pop(0)

        @pl.when(pl.program_id(1) == 0)
        def _():
            x = x_ref[...].astype(F32)
            if has_gain:
                x = _rms(x, g_ref[...])
            xs_ref[...] = x.astype(BF16)

        xv = xs_ref[...]
    else:
        xv = x_ref[...]
    acc = jnp.dot(xv, w_ref[...], preferred_element_type=F32)
    if has_bias:
        acc = acc + b_ref[...]
    if has_resid:
        acc = acc + r_ref[...]
    o_ref[...] = acc.astype(o_ref.dtype)


def _linear(x, w, *, gain=None, bias=None, resid=None, out_dtype=F32, tm=512, tn=1024, name):
    m, k = x.shape
    n = w.shape[1]
    tn = min(tn, n)
    stage_x = gain is not None or x.dtype != BF16
    in_specs = [pl.BlockSpec((tm, k), lambda i, j: (i, 0)),
                pl.BlockSpec((k, tn), lambda i, j: (0, j))]
    args = [x, w]
    if gain is not None:
        in_specs.append(pl.BlockSpec((1, k), lambda i, j: (0, 0)))
        args.append(gain.reshape(1, k))
    if bias is not None:
        in_specs.append(pl.BlockSpec((1, tn), lambda i, j: (0, j)))
        args.append(bias.reshape(1, n))
    if resid is not None:
        in_specs.append(pl.BlockSpec((tm, tn), lambda i, j: (i, j)))
        args.append(resid)
    scratch = [pltpu.VMEM((tm, k), BF16)] if stage_x else []
    return pl.pallas_call(
        functools.partial(_linear_kernel, has_gain=gain is not None, has_bias=bias is not None,
                          has_resid=resid is not None, stage_x=stage_x),
        grid=(m // tm, n // tn),
        in_specs=in_specs,
        out_specs=pl.BlockSpec((tm, tn), lambda i, j: (i, j)),
        out_shape=jax.ShapeDtypeStruct((m, n), out_dtype),
        scratch_shapes=scratch,
        compiler_params=_params("parallel", "arbitrary"),
        name=name,
    )(*args)


def _inproj_kernel(x_ref, g_ref, wa_ref, wb_ref, ba_ref, bb_ref, gate_ref, rec_ref, xs_ref):
    @pl.when(pl.program_id(1) == 0)
    def _():
        xs_ref[...] = _rms(x_ref[...], g_ref[...]).astype(BF16)

    xv = xs_ref[...]
    ga = jnp.dot(xv, wa_ref[...], preferred_element_type=F32) + ba_ref[...]
    gate_ref[...] = jax.nn.gelu(ga).astype(gate_ref.dtype)
    rec_ref[...] = jnp.dot(xv, wb_ref[...], preferred_element_type=F32) + bb_ref[...]


def _inproj(x, gain, w_in, b_in, *, tm=512, tn=1024):
    nj = D_RNN // tn
    b2 = b_in.reshape(1, 2 * D_RNN)
    return pl.pallas_call(
        _inproj_kernel,
        grid=(TOKENS // tm, nj),
        in_specs=[pl.BlockSpec((tm, D_MODEL), lambda i, j: (i, 0)),
                  pl.BlockSpec((1, D_MODEL), lambda i, j: (0, 0)),
                  pl.BlockSpec((D_MODEL, tn), lambda i, j: (0, j)),
                  pl.BlockSpec((D_MODEL, tn), lambda i, j: (0, j + nj)),
                  pl.BlockSpec((1, tn), lambda i, j: (0, j)),
                  pl.BlockSpec((1, tn), lambda i, j: (0, j + nj))],
        out_specs=[pl.BlockSpec((tm, tn), lambda i, j: (i, j)),
                   pl.BlockSpec((tm, tn), lambda i, j: (i, j))],
        out_shape=[jax.ShapeDtypeStruct((TOKENS, D_RNN), BF16),
                   jax.ShapeDtypeStruct((TOKENS, D_RNN), F32)],
        scratch_shapes=[pltpu.VMEM((tm, D_MODEL), BF16)],
        compiler_params=_params("parallel", "arbitrary"),
        name="lru_in_proj",
    )(x, gain.reshape(1, D_MODEL), w_in, w_in, b2, b2)


def _sigmoid(x):
    return 0.5 * jnp.tanh(0.5 * x) + 0.5


def _rglru_kernel(x_ref, gate_ref, cw_ref, cb_ref, wg_ref, bg_ref, lam_ref, y_ref,
                  xp_s, a_s, b_s, tail_s, eq_s, cm_s, hc_s, *, chunk):
    c = pl.program_id(1)
    seg = chunk // SUBLANES
    halo = SUBLANES * (CONV_W - 1)

    @pl.when(c == 0)
    def _():
        tail_s[...] = jnp.zeros_like(tail_s)
        hc_s[...] = jnp.zeros_like(hc_s)

    for g in range(LANE_GROUPS):
        lanes = slice(g * LANES, (g + 1) * LANES)
        for j in range(SUBLANES):
            xp_s[g, pl.ds(halo + j, seg, stride=SUBLANES), :] = x_ref[j * seg:(j + 1) * seg, lanes]

    first_sublane = lax.broadcasted_iota(jnp.int32, (SUBLANES, LANES), 0) == 0
    for g in range(LANE_GROUPS):
        for d in range(1, CONV_W):
            slot = slice(SUBLANES * (CONV_W - 1 - d), SUBLANES * (CONV_W - d))
            cur = xp_s[g, halo + SUBLANES * (seg - d):halo + SUBLANES * (seg - d + 1), :]
            prev = tail_s[g, slot, :]
            xp_s[g, slot, :] = jnp.where(first_sublane, pltpu.roll(prev, 1, 0),
                                         pltpu.roll(cur, 1, 0))
            tail_s[g, slot, :] = cur

    lam = lam_ref[...]
    sp = jnp.maximum(-lam, 0.0) + jnp.log1p(jnp.exp(-jnp.abs(lam)))
    per_blk = LRU_BLOCK // LANES
    for n in range(N_BLK):
        parts = []
        for sub in range(per_blk):
            g = n * per_blk + sub
            lanes = slice(g * LANES, (g + 1) * LANES)
            acc = cb_ref[:, lanes] + cw_ref[CONV_W - 1:CONV_W, lanes] * xp_s[g, halo:halo + chunk, :]
            for d in range(1, CONV_W):
                off = halo - SUBLANES * d
                acc = acc + cw_ref[CONV_W - 1 - d:CONV_W - d, lanes] * xp_s[g, off:off + chunk, :]
            parts.append(acc)
        xn = jnp.concatenate(parts, axis=1)
        cols = slice(n * LRU_BLOCK, (n + 1) * LRU_BLOCK)
        gt = jnp.dot(xn.astype(BF16), wg_ref[n], preferred_element_type=F32) + bg_ref[n:n + 1, :]
        r = _sigmoid(gt[:, :LRU_BLOCK])
        i = _sigmoid(gt[:, LRU_BLOCK:])
        log_a = (-LRU_C) * r * sp[:, cols]
        a = jnp.exp(log_a)
        mult = jnp.sqrt(-jnp.tanh(log_a) * (a * a + 1.0))
        b = mult * (i * xn)
        for sub in range(per_blk):
            lanes = slice(sub * LANES, (sub + 1) * LANES)
            a_s[n * per_blk + sub] = a[:, lanes]
            b_s[n * per_blk + sub] = b[:, lanes]

    def tile(ref, g, s):
        return ref[g, pl.ds(pl.multiple_of(s * SUBLANES, SUBLANES), SUBLANES), :]

    for g0 in range(0, LANE_GROUPS, SCAN_GROUPS):
        groups = range(g0, g0 + SCAN_GROUPS)

        def pass1(s, carry):
            hs, ps = carry
            a = [tile(a_s, g, s) for g in groups]
            b = [tile(b_s, g, s) for g in groups]
            return (tuple(ai * hi + bi for ai, hi, bi in zip(a, hs, b)),
                    tuple(ai * pi for ai, pi in zip(a, ps)))

        zero = jnp.zeros((SUBLANES, LANES), F32)
        one = jnp.ones((SUBLANES, LANES), F32)
        e, q = lax.fori_loop(0, seg, pass1, ((zero,) * SCAN_GROUPS, (one,) * SCAN_GROUPS),
                             unroll=8)
        for idx, g in enumerate(groups):
            eq_s[0] = e[idx]
            eq_s[1] = q[idx]
            h = hc_s[g, 0:1, :]
            for j in range(SUBLANES):
                cm_s[g, j:j + 1, :] = h
                h = eq_s[1, j:j + 1, :] * h + eq_s[0, j:j + 1, :]
            hc_s[g, 0:1, :] = h

        def pass2(s, hs):
            out = []
            for g, h in zip(groups, hs):
                h = tile(a_s, g, s) * h + tile(b_s, g, s)
                b_s[g, pl.ds(pl.multiple_of(s * SUBLANES, SUBLANES), SUBLANES), :] = h
                out.append(h)
            return tuple(out)

        lax.fori_loop(0, seg, pass2, tuple(cm_s[g] for g in groups), unroll=8)

    for g in range(LANE_GROUPS):
        lanes = slice(g * LANES, (g + 1) * LANES)
        for j in range(SUBLANES):
            rows = slice(j * seg, (j + 1) * seg)
            h = b_s[g, pl.ds(j, seg, stride=SUBLANES), :]
            y_ref[rows, lanes] = (gate_ref[rows, lanes].astype(F32) * h).astype(y_ref.dtype)


def _rglru(rec_pre, gate, conv_w, conv_b, w_gate, b_gate, lam, *, chunk=512):
    nc = SEQ // chunk
    row = lambda b, c: (b * nc + c, 0)
    full2 = lambda b, c: (0, 0)
    return pl.pallas_call(
        functools.partial(_rglru_kernel, chunk=chunk),
        grid=(BATCH, nc),
        in_specs=[pl.BlockSpec((chunk, D_RNN), row),
                  pl.BlockSpec((chunk, D_RNN), row),
                  pl.BlockSpec((CONV_W, D_RNN), full2),
                  pl.BlockSpec((1, D_RNN), full2),
                  pl.BlockSpec((N_BLK, LRU_BLOCK, 2 * LRU_BLOCK), lambda b, c: (0, 0, 0)),
                  pl.BlockSpec((N_BLK, 2 * LRU_BLOCK), full2),
                  pl.BlockSpec((1, D_RNN), full2)],
        out_specs=pl.BlockSpec((chunk, D_RNN), row),
        out_shape=jax.ShapeDtypeStruct((TOKENS, D_RNN), BF16),
        scratch_shapes=[pltpu.VMEM((LANE_GROUPS, SUBLANES * (CONV_W - 1) + chunk, LANES), F32),
                        pltpu.VMEM((LANE_GROUPS, chunk, LANES), F32),
                        pltpu.VMEM((LANE_GROUPS, chunk, LANES), F32),
                        pltpu.VMEM((LANE_GROUPS, SUBLANES * (CONV_W - 1), LANES), F32),
                        pltpu.VMEM((2, SUBLANES, LANES), F32),
                        pltpu.VMEM((LANE_GROUPS, SUBLANES, LANES), F32),
                        pltpu.VMEM((LANE_GROUPS, SUBLANES, LANES), F32)],
        compiler_params=_params("parallel", "arbitrary"),
        name="rglru_core",
    )(rec_pre, gate, conv_w, conv_b.reshape(1, D_RNN), w_gate, b_gate, lam.reshape(1, D_RNN))


def _ffn_kernel(h_ref, g_ref, wa_ref, wb_ref, w2_ref, o_ref, hn_s):
    f = pl.program_id(1)

    @pl.when(f == 0)
    def _():
        h = h_ref[...]
        hn_s[...] = _rms(h, g_ref[...]).astype(BF16)
        o_ref[...] = h

    hn = hn_s[...]
    ua = jnp.dot(hn, wa_ref[...], preferred_element_type=F32)
    ub = jnp.dot(hn, wb_ref[...], preferred_element_type=F32)
    act = (jax.nn.silu(ua) * ub).astype(BF16)
    o_ref[...] += jnp.dot(act, w2_ref[...], preferred_element_type=F32)


def _ffn(h, gain, w_in, w_out, *, tm=512, tf=512):
    nf = D_FF // tf
    return pl.pallas_call(
        _ffn_kernel,
        grid=(TOKENS // tm, nf),
        in_specs=[pl.BlockSpec((tm, D_MODEL), lambda i, f: (i, 0)),
                  pl.BlockSpec((1, D_MODEL), lambda i, f: (0, 0)),
                  pl.BlockSpec((D_MODEL, tf), lambda i, f: (0, f)),
                  pl.BlockSpec((D_MODEL, tf), lambda i, f: (0, f + nf)),
                  pl.BlockSpec((tf, D_MODEL), lambda i, f: (f, 0))],
        out_specs=pl.BlockSpec((tm, D_MODEL), lambda i, f: (i, 0)),
        out_shape=jax.ShapeDtypeStruct((TOKENS, D_MODEL), F32),
        scratch_shapes=[pltpu.VMEM((tm, D_MODEL), BF16)],
        compiler_params=_params("parallel", "arbitrary"),
        name="swiglu_ffn",
    )(h, gain.reshape(1, D_MODEL), w_in, w_in, w_out)


def _kv_heads_kernel(c_ref, kr_ref, gl_ref, w_ref, gn_ref, gr_ref, rc_ref, rs_ref,
                     k_ref, v_ref, cn_s, krr_s, ssr_s, *, heads):
    @pl.when(pl.program_id(1) == 0)
    def _():
        cn_s[...] = _rms(c_ref[...], gl_ref[...]).astype(BF16)
        kr = kr_ref[...]
        ssr_s[...] = jnp.broadcast_to(jnp.sum(kr * kr, axis=1, keepdims=True), ssr_s.shape)
        krr_s[...] = _rope(kr * gr_ref[...], rc_ref[...], rs_ref[...])

    ones = jnp.ones((v_ref.shape[0], HEAD_PAD - V_D), v_ref.dtype)
    for g in range(heads):
        base = g * HEAD_PAD
        kv = jnp.dot(cn_s[...], w_ref[:, base:base + HEAD_PAD], preferred_element_type=F32)
        kn = kv[:, :NOPE_D]
        ss = jnp.sum(kn * kn, axis=1, keepdims=True) + ssr_s[:, 0:1]
        rstd = lax.rsqrt(ss * (1.0 / QK_D) + NORM_EPS)
        k_ref[:, base:base + NOPE_D] = (kn * rstd * gn_ref[...]).astype(k_ref.dtype)
        k_ref[:, base + NOPE_D:base + HEAD_PAD] = (krr_s[...] * rstd).astype(k_ref.dtype)
        v_ref[:, base:base + V_D] = kv[:, NOPE_D:].astype(v_ref.dtype)
        v_ref[:, base + V_D:base + HEAD_PAD] = ones


def _kv_heads(ckr, gl, w_ukv, gn, gr, rc, rs, *, tm=512, heads=4):
    rope_blk = KV_LORA // LANES
    row = lambda i, h: (i, 0)
    vec = lambda i, h: (0, 0)
    return pl.pallas_call(
        functools.partial(_kv_heads_kernel, heads=heads),
        grid=(TOKENS // tm, N_HEADS // heads),
        in_specs=[pl.BlockSpec((tm, KV_LORA), row),
                  pl.BlockSpec((tm, LANES), lambda i, h: (i, rope_blk)),
                  pl.BlockSpec((1, KV_LORA), vec),
                  pl.BlockSpec((KV_LORA, heads * HEAD_PAD), lambda i, h: (0, h)),
                  pl.BlockSpec((1, LANES), vec),
                  pl.BlockSpec((1, LANES), vec),
                  pl.BlockSpec((tm, LANES), row),
                  pl.BlockSpec((tm, LANES), row)],
        out_specs=[pl.BlockSpec((tm, heads * HEAD_PAD), lambda i, h: (i, h)),
                   pl.BlockSpec((tm, heads * HEAD_PAD), lambda i, h: (i, h))],
        out_shape=[jax.ShapeDtypeStruct((TOKENS, N_HEADS * HEAD_PAD), BF16),
                   jax.ShapeDtypeStruct((TOKENS, N_HEADS * HEAD_PAD), BF16)],
        scratch_shapes=[pltpu.VMEM((tm, KV_LORA), BF16),
                        pltpu.VMEM((tm, LANES), F32),
                        pltpu.VMEM((tm, LANES), F32)],
        compiler_params=_params("parallel", "arbitrary"),
        name="kv_heads",
    )(ckr, ckr, gl.reshape(1, KV_LORA), w_ukv, gn, gr, rc, rs)


def _q_heads_kernel(c_ref, gl_ref, w_ref, gn_ref, gr_ref, gs_ref, rc_ref, rs_ref,
                    q_ref, cn_s, cg_s, sg_s, *, heads):
    @pl.when(pl.program_id(1) == 0)
    def _():
        cn_s[...] = _rms(c_ref[...], gl_ref[...]).astype(BF16)
        cg_s[...] = rc_ref[...] * gr_ref[...]
        sg_s[...] = rs_ref[...] * gs_ref[...]

    for g in range(heads):
        wbase = g * Q_HEAD_COLS
        q = jnp.dot(cn_s[...], w_ref[:, wbase:wbase + Q_HEAD_COLS], preferred_element_type=F32)
        qn = q[:, :NOPE_D]
        qr = q[:, NOPE_D:NOPE_D + LANES]
        qs = q[:, NOPE_D + LANES:]
        ss = jnp.sum(qn * qn + qr * qr, axis=1, keepdims=True)
        rstd = lax.rsqrt(ss * (1.0 / QK_D) + NORM_EPS)
        base = g * HEAD_PAD
        q_ref[:, base:base + NOPE_D] = (qn * rstd * gn_ref[...]).astype(q_ref.dtype)
        roped = qr * cg_s[...] + qs * sg_s[...]
        q_ref[:, base + NOPE_D:base + HEAD_PAD] = (roped * rstd).astype(q_ref.dtype)


def _q_heads(cq, gl, w_uq3, gn, gr, gs, rc, rs, *, tm=512, heads=4):
    row = lambda i, h: (i, 0)
    vec = lambda i, h: (0, 0)
    return pl.pallas_call(
        functools.partial(_q_heads_kernel, heads=heads),
        grid=(TOKENS // tm, N_HEADS // heads),
        in_specs=[pl.BlockSpec((tm, Q_LORA), row),
                  pl.BlockSpec((1, Q_LORA), vec),
                  pl.BlockSpec((Q_LORA, heads * Q_HEAD_COLS), lambda i, h: (0, h)),
                  pl.BlockSpec((1, LANES), vec),
                  pl.BlockSpec((1, LANES), vec),
                  pl.BlockSpec((1, LANES), vec),
                  pl.BlockSpec((tm, LANES), row),
                  pl.BlockSpec((tm, LANES), row)],
        out_specs=pl.BlockSpec((tm, heads * HEAD_PAD), lambda i, h: (i, h)),
        out_shape=jax.ShapeDtypeStruct((TOKENS, N_HEADS * HEAD_PAD), BF16),
        scratch_shapes=[pltpu.VMEM((tm, Q_LORA), BF16),
                        pltpu.VMEM((tm, LANES), F32),
                        pltpu.VMEM((tm, LANES), F32)],
        compiler_params=_params("parallel", "arbitrary"),
        name="q_heads",
    )(cq, gl.reshape(1, Q_LORA), w_uq3, gn, gr, gs, rc, rs)


def _attn_kernel(q_ref, k_ref, v_ref, o_ref, m_s, acc_s, *, blk, heads):
    i = pl.program_id(2)
    m_s[...] = jnp.full_like(m_s, NEG_BIG)
    acc_s[...] = jnp.zeros_like(acc_s)

    def step(j, masked):
        start = pl.multiple_of(j * blk, blk)
        for g in range(heads):
            qk_cols = slice(g * HEAD_PAD, (g + 1) * HEAD_PAD)
            k = k_ref[pl.ds(start, blk), qk_cols]
            v = v_ref[pl.ds(start, blk), qk_cols]
            s = lax.dot_general(q_ref[:, qk_cols], k, (((1,), (1,)), ((), ())),
                                preferred_element_type=F32)
            if masked:
                row = lax.broadcasted_iota(jnp.int32, (blk, blk), 0)
                col = lax.broadcasted_iota(jnp.int32, (blk, blk), 1)
                s = jnp.where(col <= row, s, NEG_BIG)
            m_prev = m_s[g]
            m_new = jnp.maximum(m_prev, jnp.max(s, axis=1, keepdims=True))
            p = jnp.exp2(s - pltpu.repeat(m_new, blk // LANES, axis=1))
            alpha = jnp.exp2(m_prev - m_new)
            pv = jnp.dot(p.astype(BF16), v, preferred_element_type=F32)
            acc_s[g] = pltpu.repeat(alpha, HEAD_PAD // LANES, axis=1) * acc_s[g] + pv
            m_s[g] = m_new

    def body(j, carry):
        step(j, False)
        return carry

    lax.fori_loop(0, i, body, 0)
    step(i, True)
    for g in range(heads):
        acc = acc_s[g]
        o_ref[:, g * V_D:(g + 1) * V_D] = (acc[:, :V_D] / acc[:, V_D:]).astype(o_ref.dtype)


def _attention(q, k, v, *, blk=512, heads=4):
    nq = SEQ // blk
    return pl.pallas_call(
        functools.partial(_attn_kernel, blk=blk, heads=heads),
        grid=(BATCH, N_HEADS // heads, nq),
        in_specs=[pl.BlockSpec((blk, heads * HEAD_PAD), lambda b, h, i: (b * nq + i, h)),
                  pl.BlockSpec((SEQ, heads * HEAD_PAD), lambda b, h, i: (b, h)),
                  pl.BlockSpec((SEQ, heads * HEAD_PAD), lambda b, h, i: (b, h))],
        out_specs=pl.BlockSpec((blk, heads * V_D), lambda b, h, i: (b * nq + i, h)),
        out_shape=jax.ShapeDtypeStruct((TOKENS, N_HEADS * V_D), BF16),
        scratch_shapes=[pltpu.VMEM((heads, blk, LANES), F32),
                        pltpu.VMEM((heads, blk, HEAD_PAD), F32)],
        compiler_params=_params("parallel", "parallel", "arbitrary"),
        name="flash_attention",
    )(q, k, v)


def _swap_halves(t):
    half = ROPE_D // 2
    return jnp.concatenate([t[..., half:], t[..., :half]], axis=-1)


def _split_qk_gain(g, scale):
    pad = lambda v: jnp.pad(v * scale, (0, LANES - ROPE_D)).reshape(1, LANES)
    return (g[:NOPE_D] * scale).reshape(1, LANES), pad(g[NOPE_D:]), pad(_swap_halves(g[NOPE_D:]))


def kernel(x, positions, norm_mix, norm_ffn, lru_w_in, lru_b_in, lru_conv_w, lru_conv_b, lru_w_gate, lru_b_gate, lru_lambda, lru_w_out, lru_b_out, kv_norm_in, w_dkv, kv_latent_norm, w_ukv, k_norm, w_dq, q_latent_norm, w_uq, q_norm, w_o, ffn_w_in, ffn_w_out):
    h = x.reshape(TOKENS, D_MODEL)
    rc, rs = _rope_tables(positions)

    gate, rec_pre = _inproj(h, norm_mix[0], lru_w_in[0].astype(BF16), lru_b_in[0])
    y = _rglru(rec_pre, gate, lru_conv_w[0], lru_conv_b[0], lru_w_gate[0].astype(BF16),
               lru_b_gate[0], lru_lambda[0])
    h = _linear(y, lru_w_out[0].astype(BF16), bias=lru_b_out[0], resid=h, tn=D_MODEL,
                name="lru_out_proj")
    h = _ffn(h, norm_ffn[0], ffn_w_in[0].astype(BF16), ffn_w_out[0].astype(BF16))

    w_dkv_p = jnp.pad(w_dkv, ((0, 0), (0, LANES - ROPE_D))).astype(BF16)
    ckr = _linear(h, w_dkv_p, gain=kv_norm_in, tn=KV_LORA + LANES, name="kv_down_proj")
    kgn, kgr, _ = _split_qk_gain(k_norm, 1.0)
    k_sh, v_sh = _kv_heads(ckr, kv_latent_norm,
                           w_ukv.reshape(KV_LORA, N_HEADS * (NOPE_D + V_D)).astype(BF16),
                           kgn, kgr, rc, rs)

    cq = _linear(h, w_dq[0].astype(BF16), gain=norm_mix[1], tn=Q_LORA, name="q_down_proj")
    zpad = jnp.zeros((Q_LORA, N_HEADS, LANES - ROPE_D), F32)
    w_rope = w_uq[0][..., NOPE_D:]
    w_uq3 = jnp.concatenate([w_uq[0][..., :NOPE_D], w_rope, zpad, _swap_halves(w_rope), zpad],
                            axis=-1)
    qgn, qgr, qgs = _split_qk_gain(q_norm[0], math.log2(math.e) / math.sqrt(QK_D))
    q = _q_heads(cq, q_latent_norm[0],
                 w_uq3.reshape(Q_LORA, N_HEADS * Q_HEAD_COLS).astype(BF16),
                 qgn, qgr, qgs, rc, rs)
    o = _attention(q, k_sh, v_sh)
    h = _linear(o, w_o[0].astype(BF16), resid=h, tn=D_MODEL, name="attn_out_proj")
    h = _ffn(h, norm_ffn[1], ffn_w_in[1].astype(BF16), ffn_w_out[1].astype(BF16))
    return h.reshape(BATCH, SEQ, D_MODEL)
```

```python
import functools
import math

import jax
import jax.numpy as jnp
from jax import lax
from jax.experimental import pallas as pl
from jax.experimental.pallas import tpu as pltpu

F32 = jnp.float32
BF16 = jnp.bfloat16

D_MODEL = 2048
BATCH = 4
SEQ = 4096
TOKENS = BATCH * SEQ
D_RNN = D_MODEL
LRU_BLOCK = 256
N_BLK = D_RNN // LRU_BLOCK
CONV_W = 4
LRU_C = 8.0
N_HEADS = 16
NOPE_D = 128
ROPE_D = 64
QK_D = NOPE_D + ROPE_D
V_D = 128
Q_LORA = 512
KV_LORA = 512
ROPE_THETA = 10000.0
D_FF = 5632
NORM_EPS = 1e-6

LANES = 128
SUBLANES = 8
HEAD_PAD = 2 * LANES
Q_HEAD_COLS = 3 * LANES
LANE_GROUPS = D_RNN // LANES
SCAN_GROUPS = 8
V7X_VMEM_BYTES = 64 * 1024 * 1024
VMEM_LIMIT = V7X_VMEM_BYTES * 7 // 8
NEG_BIG = -1e30


def _params(*semantics):
    return pltpu.CompilerParams(dimension_semantics=semantics, vmem_limit_bytes=VMEM_LIMIT)


def _rms(x, g):
    ms = jnp.mean(x * x, axis=-1, keepdims=True)
    return x * lax.rsqrt(ms + NORM_EPS) * g


def _rope(t, c, s):
    swapped = pltpu.roll(t, LANES - ROPE_D // 2, 1) + pltpu.roll(t, ROPE_D // 2, 1)
    return t * c + swapped * s


def _rope_table_kernel(pos_ref, invf_ref, cos_ref, sin_ref):
    ang = pos_ref[...].astype(F32) * invf_ref[...]
    cos_ref[...] = jnp.cos(ang)
    sin_ref[...] = jnp.sin(ang)


def _rope_tables(positions):
    half = ROPE_D // 2
    per_row = LANES // half
    rows = TOKENS // per_row
    inv_freq = ROPE_THETA ** (-jnp.arange(0, ROPE_D, 2, dtype=F32) / ROPE_D)
    pos = jnp.repeat(positions.reshape(rows, per_row), half, axis=1)
    invf = jnp.tile(inv_freq, per_row).reshape(1, LANES)
    tr = 512
    cosp, sinp = pl.pallas_call(
        _rope_table_kernel,
        grid=(rows // tr,),
        in_specs=[pl.BlockSpec((tr, LANES), lambda i: (i, 0)),
                  pl.BlockSpec((1, LANES), lambda i: (0, 0))],
        out_specs=[pl.BlockSpec((tr, LANES), lambda i: (i, 0)),
                   pl.BlockSpec((tr, LANES), lambda i: (i, 0))],
        out_shape=[jax.ShapeDtypeStruct((rows, LANES), F32)] * 2,
        compiler_params=_params("arbitrary"),
        name="rope_tables",
    )(pos, invf)
    cos = cosp.reshape(TOKENS, half)
    sin = sinp.reshape(TOKENS, half)
    z = jnp.zeros((TOKENS, LANES - ROPE_D), F32)
    rc = jnp.concatenate([cos, cos, z], axis=1)
    rs = jnp.concatenate([-sin, sin, z], axis=1)
    return rc, rs


def _linear_kernel(*refs, has_gain, has_bias, has_resid, stage_x):
    refs = list(refs)
    x_ref = refs.pop(0)
    w_ref = refs.pop(0)
    g_ref = refs.pop(0) if has_gain else None
    b_ref = refs.pop(0) if has_bias else None
    r_ref = refs.pop(0) if has_resid else None
    o_ref = refs.pop(0)
    if stage_x:
        xs_ref = refs.pop(0)

        @pl.when(pl.program_id(1) == 0)
        def _():
            x = x_ref[...].astype(F32)
            if has_gain:
                x = _rms(x, g_ref[...])
            xs_ref[...] = x.astype(BF16)

        xv = xs_ref[...]
    else:
        xv = x_ref[...]
    acc = jnp.dot(xv, w_ref[...], preferred_element_type=F32)
    if has_bias:
        acc = acc + b_ref[...]
    if has_resid:
        acc = acc + r_ref[...]
    o_ref[...] = acc.astype(o_ref.dtype)


def _linear(x, w, *, gain=None, bias=None, resid=None, out_dtype=F32, tm=512, tn=1024, name):
    m, k = x.shape
    n = w.shape[1]
    tn = min(tn, n)
    stage_x = gain is not None or x.dtype != BF16
    in_specs = [pl.BlockSpec((tm, k), lambda i, j: (i, 0)),
                pl.BlockSpec((k, tn), lambda i, j: (0, j))]
    args = [x, w]
    if gain is not None:
        in_specs.append(pl.BlockSpec((1, k), lambda i, j: (0, 0)))
        args.append(gain.reshape(1, k))
    if bias is not None:
        in_specs.append(pl.BlockSpec((1, tn), lambda i, j: (0, j)))
        args.append(bias.reshape(1, n))
    if resid is not None:
        in_specs.append(pl.BlockSpec((tm, tn), lambda i, j: (i, j)))
        args.append(resid)
    scratch = [pltpu.VMEM((tm, k), BF16)] if stage_x else []
    return pl.pallas_call(
        functools.partial(_linear_kernel, has_gain=gain is not None, has_bias=bias is not None,
                          has_resid=resid is not None, stage_x=stage_x),
        grid=(m // tm, n // tn),
        in_specs=in_specs,
        out_specs=pl.BlockSpec((tm, tn), lambda i, j: (i, j)),
        out_shape=jax.ShapeDtypeStruct((m, n), out_dtype),
        scratch_shapes=scratch,
        compiler_params=_params("parallel", "arbitrary"),
        name=name,
    )(*args)


def _down_proj_kernel(x_ref, gkv_ref, gq_ref, wkv_ref, wq_ref, ckr_ref, cq_ref):
    x = x_ref[...]
    xhat = x * lax.rsqrt(jnp.mean(x * x, axis=-1, keepdims=True) + NORM_EPS)
    ckr_ref[...] = jnp.dot((xhat * gkv_ref[...]).astype(BF16), wkv_ref[...],
                           preferred_element_type=F32)
    cq_ref[...] = jnp.dot((xhat * gq_ref[...]).astype(BF16), wq_ref[...],
                          preferred_element_type=F32)


def _down_proj(x, g_kv, g_q, w_dkv, w_dq, *, tm=1024):
    n_kv = w_dkv.shape[1]
    row = lambda i: (i, 0)
    full = lambda i: (0, 0)
    return pl.pallas_call(
        _down_proj_kernel,
        grid=(TOKENS // tm,),
        in_specs=[pl.BlockSpec((tm, D_MODEL), row),
                  pl.BlockSpec((1, D_MODEL), full),
                  pl.BlockSpec((1, D_MODEL), full),
                  pl.BlockSpec((D_MODEL, n_kv), full),
                  pl.BlockSpec((D_MODEL, Q_LORA), full)],
        out_specs=[pl.BlockSpec((tm, n_kv), row),
                   pl.BlockSpec((tm, Q_LORA), row)],
        out_shape=[jax.ShapeDtypeStruct((TOKENS, n_kv), F32),
                   jax.ShapeDtypeStruct((TOKENS, Q_LORA), F32)],
        compiler_params=_params("parallel"),
        name="mla_down_proj",
    )(x, g_kv.reshape(1, D_MODEL), g_q.reshape(1, D_MODEL), w_dkv, w_dq)


def _inproj_kernel(x_ref, g_ref, wa_ref, wb_ref, ba_ref, bb_ref, gate_ref, rec_ref, xs_ref):
    @pl.when(pl.program_id(1) == 0)
    def _():
        xs_ref[...] = _rms(x_ref[...], g_ref[...]).astype(BF16)

    xv = xs_ref[...]
    ga = jnp.dot(xv, wa_ref[...], preferred_element_type=F32) + ba_ref[...]
    gate_ref[...] = jax.nn.gelu(ga).astype(gate_ref.dtype)
    rec_ref[...] = jnp.dot(xv, wb_ref[...], preferred_element_type=F32) + bb_ref[...]


def _inproj(x, gain, w_in, b_in, *, tm=512, tn=1024):
    nj = D_RNN // tn
    b2 = b_in.reshape(1, 2 * D_RNN)
    return pl.pallas_call(
        _inproj_kernel,
        grid=(TOKENS // tm, nj),
        in_specs=[pl.BlockSpec((tm, D_MODEL), lambda i, j: (i, 0)),
                  pl.BlockSpec((1, D_MODEL), lambda i, j: (0, 0)),
                  pl.BlockSpec((D_MODEL, tn), lambda i, j: (0, j)),
                  pl.BlockSpec((D_MODEL, tn), lambda i, j: (0, j + nj)),
                  pl.BlockSpec((1, tn), lambda i, j: (0, j)),
                  pl.BlockSpec((1, tn), lambda i, j: (0, j + nj))],
        out_specs=[pl.BlockSpec((tm, tn), lambda i, j: (i, j)),
                   pl.BlockSpec((tm, tn), lambda i, j: (i, j))],
        out_shape=[jax.ShapeDtypeStruct((TOKENS, D_RNN), BF16),
                   jax.ShapeDtypeStruct((TOKENS, D_RNN), F32)],
        scratch_shapes=[pltpu.VMEM((tm, D_MODEL), BF16)],
        compiler_params=_params("parallel", "arbitrary"),
        name="lru_in_proj",
    )(x, gain.reshape(1, D_MODEL), w_in, w_in, b2, b2)


def _sigmoid(x):
    return 0.5 * jnp.tanh(0.5 * x) + 0.5


def _rglru_kernel(x_ref, gate_ref, cw_ref, cb_ref, wg_ref, bg_ref, lam_ref, y_ref,
                  xp_s, a_s, b_s, tail_s, eq_s, cm_s, hc_s, *, chunk):
    c = pl.program_id(1)
    seg = chunk // SUBLANES
    halo = SUBLANES * (CONV_W - 1)

    @pl.when(c == 0)
    def _():
        tail_s[...] = jnp.zeros_like(tail_s)
        hc_s[...] = jnp.zeros_like(hc_s)

    for g in range(LANE_GROUPS):
        lanes = slice(g * LANES, (g + 1) * LANES)
        for j in range(SUBLANES):
            xp_s[g, pl.ds(halo + j, seg, stride=SUBLANES), :] = x_ref[j * seg:(j + 1) * seg, lanes]

    first_sublane = lax.broadcasted_iota(jnp.int32, (SUBLANES, LANES), 0) == 0
    for g in range(LANE_GROUPS):
        for d in range(1, CONV_W):
            slot = slice(SUBLANES * (CONV_W - 1 - d), SUBLANES * (CONV_W - d))
            cur = xp_s[g, halo + SUBLANES * (seg - d):halo + SUBLANES * (seg - d + 1), :]
            prev = tail_s[g, slot, :]
            xp_s[g, slot, :] = jnp.where(first_sublane, pltpu.roll(prev, 1, 0),
                                         pltpu.roll(cur, 1, 0))
            tail_s[g, slot, :] = cur

    lam = lam_ref[...]
    sp = jnp.maximum(-lam, 0.0) + jnp.log1p(jnp.exp(-jnp.abs(lam)))
    per_blk = LRU_BLOCK // LANES
    for n in range(N_BLK):
        parts = []
        for sub in range(per_blk):
            g = n * per_blk + sub
            lanes = slice(g * LANES, (g + 1) * LANES)
            acc = cb_ref[:, lanes] + cw_ref[CONV_W - 1:CONV_W, lanes] * xp_s[g, halo:halo + chunk, :]
            for d in range(1, CONV_W):
                off = halo - SUBLANES * d
                acc = acc + cw_ref[CONV_W - 1 - d:CONV_W - d, lanes] * xp_s[g, off:off + chunk, :]
            parts.append(acc)
        xn = jnp.concatenate(parts, axis=1)
        cols = slice(n * LRU_BLOCK, (n + 1) * LRU_BLOCK)
        gt = jnp.dot(xn.astype(BF16), wg_ref[n], preferred_element_type=F32) + bg_ref[n:n + 1, :]
        r = _sigmoid(gt[:, :LRU_BLOCK])
        i = _sigmoid(gt[:, LRU_BLOCK:])
        log_a = (-LRU_C) * r * sp[:, cols]
        a = jnp.exp(log_a)
        mult = jnp.sqrt(-jnp.tanh(log_a) * (a * a + 1.0))
        b = mult * (i * xn)
        for sub in range(per_blk):
            lanes = slice(sub * LANES, (sub + 1) * LANES)
            a_s[n * per_blk + sub] = a[:, lanes]
            b_s[n * per_blk + sub] = b[:, lanes]

    def tile(ref, g, s):
        return ref[g, pl.ds(pl.multiple_of(s * SUBLANES, SUBLANES), SUBLANES), :]

    for g0 in range(0, LANE_GROUPS, SCAN_GROUPS):
        groups = range(g0, g0 + SCAN_GROUPS)

        def pass1(s, carry):
            hs, ps = carry
            a = [tile(a_s, g, s) for g in groups]
            b = [tile(b_s, g, s) for g in groups]
            return (tuple(ai * hi + bi for ai, hi, bi in zip(a, hs, b)),
                    tuple(ai * pi for ai, pi in zip(a, ps)))

        zero = jnp.zeros((SUBLANES, LANES), F32)
        one = jnp.ones((SUBLANES, LANES), F32)
        e, q = lax.fori_loop(0, seg, pass1, ((zero,) * SCAN_GROUPS, (one,) * SCAN_GROUPS),
                             unroll=8)
        for idx, g in enumerate(groups):
            eq_s[0] = e[idx]
            eq_s[1] = q[idx]
            h = hc_s[g, 0:1, :]
            for j in range(SUBLANES):
                cm_s[g, j:j + 1, :] = h
                h = eq_s[1, j:j + 1, :] * h + eq_s[0, j:j + 1, :]
            hc_s[g, 0:1, :] = h

        def pass2(s, hs):
            out = []
            for g, h in zip(groups, hs):
                h = tile(a_s, g, s) * h + tile(b_s, g, s)
                b_s[g, pl.ds(pl.multiple_of(s * SUBLANES, SUBLANES), SUBLANES), :] = h
                out.append(h)
            return tuple(out)

        lax.fori_loop(0, seg, pass2, tuple(cm_s[g] for g in groups), unroll=8)

    for g in range(LANE_GROUPS):
        lanes = slice(g * LANES, (g + 1) * LANES)
        for j in range(SUBLANES):
            rows = slice(j * seg, (j + 1) * seg)
            h = b_s[g, pl.ds(j, seg, stride=SUBLANES), :]
            y_ref[rows, lanes] = (gate_ref[rows, lanes].astype(F32) * h).astype(y_ref.dtype)


def _rglru(rec_pre, gate, conv_w, conv_b, w_gate, b_gate, lam, *, chunk=512):
    nc = SEQ // chunk
    row = lambda b, c: (b * nc + c, 0)
    full2 = lambda b, c: (0, 0)
    return pl.pallas_call(
        functools.partial(_rglru_kernel, chunk=chunk),
        grid=(BATCH, nc),
        in_specs=[pl.BlockSpec((chunk, D_RNN), row),
                  pl.BlockSpec((chunk, D_RNN), row),
                  pl.BlockSpec((CONV_W, D_RNN), full2),
                  pl.BlockSpec((1, D_RNN), full2),
                  pl.BlockSpec((N_BLK, LRU_BLOCK, 2 * LRU_BLOCK), lambda b, c: (0, 0, 0)),
                  pl.BlockSpec((N_BLK, 2 * LRU_BLOCK), full2),
                  pl.BlockSpec((1, D_RNN), full2)],
        out_specs=pl.BlockSpec((chunk, D_RNN), row),
        out_shape=jax.ShapeDtypeStruct((TOKENS, D_RNN), BF16),
        scratch_shapes=[pltpu.VMEM((LANE_GROUPS, SUBLANES * (CONV_W - 1) + chunk, LANES), F32),
                        pltpu.VMEM((LANE_GROUPS, chunk, LANES), F32),
                        pltpu.VMEM((LANE_GROUPS, chunk, LANES), F32),
                        pltpu.VMEM((LANE_GROUPS, SUBLANES * (CONV_W - 1), LANES), F32),
                        pltpu.VMEM((2, SUBLANES, LANES), F32),
                        pltpu.VMEM((LANE_GROUPS, SUBLANES, LANES), F32),
                        pltpu.VMEM((LANE_GROUPS, SUBLANES, LANES), F32)],
        compiler_params=_params("parallel", "arbitrary"),
        name="rglru_core",
    )(rec_pre, gate, conv_w, conv_b.reshape(1, D_RNN), w_gate, b_gate, lam.reshape(1, D_RNN))


def _ffn_kernel(h_ref, g_ref, wa_ref, wb_ref, w2_ref, o_ref, hn_s):
    f = pl.program_id(1)

    @pl.when(f == 0)
    def _():
        h = h_ref[...]
        hn_s[...] = _rms(h, g_ref[...]).astype(BF16)
        o_ref[...] = h

    hn = hn_s[...]
    ua = jnp.dot(hn, wa_ref[...], preferred_element_type=F32)
    ub = jnp.dot(hn, wb_ref[...], preferred_element_type=F32)
    act = (jax.nn.silu(ua) * ub).astype(BF16)
    o_ref[...] += jnp.dot(act, w2_ref[...], preferred_element_type=F32)


def _ffn(h, gain, w_in, w_out, *, tm=1024, tf=512):
    nf = D_FF // tf
    return pl.pallas_call(
        _ffn_kernel,
        grid=(TOKENS // tm, nf),
        in_specs=[pl.BlockSpec((tm, D_MODEL), lambda i, f: (i, 0)),
                  pl.BlockSpec((1, D_MODEL), lambda i, f: (0, 0)),
                  pl.BlockSpec((D_MODEL, tf), lambda i, f: (0, f)),
                  pl.BlockSpec((D_MODEL, tf), lambda i, f: (0, f + nf)),
                  pl.BlockSpec((tf, D_MODEL), lambda i, f: (f, 0))],
        out_specs=pl.BlockSpec((tm, D_MODEL), lambda i, f: (i, 0)),
        out_shape=jax.ShapeDtypeStruct((TOKENS, D_MODEL), F32),
        scratch_shapes=[pltpu.VMEM((tm, D_MODEL), BF16)],
        compiler_params=_params("parallel", "arbitrary"),
        name="swiglu_ffn",
    )(h, gain.reshape(1, D_MODEL), w_in, w_in, w_out)


def _kv_heads_kernel(c_ref, kr_ref, gl_ref, w_ref, gn_ref, gr_ref, rc_ref, rs_ref,
                     k_ref, v_ref, cn_s, krr_s, ssr_s, *, heads):
    @pl.when(pl.program_id(1) == 0)
    def _():
        cn_s[...] = _rms(c_ref[...], gl_ref[...]).astype(BF16)
        kr = kr_ref[...]
        ssr_s[...] = jnp.broadcast_to(jnp.sum(kr * kr, axis=1, keepdims=True), ssr_s.shape)
        krr_s[...] = _rope(kr * gr_ref[...], rc_ref[...], rs_ref[...])

    ones = jnp.ones((v_ref.shape[0], HEAD_PAD - V_D), v_ref.dtype)
    kv_all = jnp.dot(cn_s[...], w_ref[...], preferred_element_type=F32)
    for g in range(heads):
        base = g * HEAD_PAD
        kv = kv_all[:, base:base + HEAD_PAD]
        kn = kv[:, :NOPE_D]
        ss = jnp.sum(kn * kn, axis=1, keepdims=True) + ssr_s[...]
        rstd = lax.rsqrt(ss * (1.0 / QK_D) + NORM_EPS)
        k_ref[:, base:base + NOPE_D] = (kn * rstd * gn_ref[...]).astype(k_ref.dtype)
        k_ref[:, base + NOPE_D:base + HEAD_PAD] = (krr_s[...] * rstd).astype(k_ref.dtype)
        v_ref[:, base:base + V_D] = kv[:, NOPE_D:].astype(v_ref.dtype)
        v_ref[:, base + V_D:base + HEAD_PAD] = ones


def _kv_heads(ckr, gl, w_ukv, gn, gr, rc, rs, *, tm=512, heads=4):
    rope_blk = KV_LORA // LANES
    row = lambda i, h: (i, 0)
    vec = lambda i, h: (0, 0)
    return pl.pallas_call(
        functools.partial(_kv_heads_kernel, heads=heads),
        grid=(TOKENS // tm, N_HEADS // heads),
        in_specs=[pl.BlockSpec((tm, KV_LORA), row),
                  pl.BlockSpec((tm, LANES), lambda i, h: (i, rope_blk)),
                  pl.BlockSpec((1, KV_LORA), vec),
                  pl.BlockSpec((KV_LORA, heads * HEAD_PAD), lambda i, h: (0, h)),
                  pl.BlockSpec((1, LANES), vec),
                  pl.BlockSpec((1, LANES), vec),
                  pl.BlockSpec((tm, LANES), row),
                  pl.BlockSpec((tm, LANES), row)],
        out_specs=[pl.BlockSpec((tm, heads * HEAD_PAD), lambda i, h: (i, h)),
                   pl.BlockSpec((tm, heads * HEAD_PAD), lambda i, h: (i, h))],
        out_shape=[jax.ShapeDtypeStruct((TOKENS, N_HEADS * HEAD_PAD), BF16),
                   jax.ShapeDtypeStruct((TOKENS, N_HEADS * HEAD_PAD), BF16)],
        scratch_shapes=[pltpu.VMEM((tm, KV_LORA), BF16),
                        pltpu.VMEM((tm, LANES), F32),
                        pltpu.VMEM((tm, LANES), F32)],
        compiler_params=_params("parallel", "arbitrary"),
        name="kv_heads",
    )(ckr, ckr, gl.reshape(1, KV_LORA), w_ukv, gn, gr, rc, rs)


def _q_heads_kernel(c_ref, gl_ref, w_ref, gn_ref, gr_ref, gs_ref, rc_ref, rs_ref,
                    q_ref, cn_s, cg_s, sg_s, *, heads):
    @pl.when(pl.program_id(1) == 0)
    def _():
        cn_s[...] = _rms(c_ref[...], gl_ref[...]).astype(BF16)
        cg_s[...] = rc_ref[...] * gr_ref[...]
        sg_s[...] = rs_ref[...] * gs_ref[...]

    q_all = jnp.dot(cn_s[...], w_ref[...], preferred_element_type=F32)
    for g in range(heads):
        q = q_all[:, g * Q_HEAD_COLS:(g + 1) * Q_HEAD_COLS]
        qn = q[:, :NOPE_D]
        qr = q[:, NOPE_D:NOPE_D + LANES]
        qs = q[:, NOPE_D + LANES:]
        ss = jnp.sum(qn * qn + qr * qr, axis=1, keepdims=True)
        rstd = lax.rsqrt(ss * (1.0 / QK_D) + NORM_EPS)
        base = g * HEAD_PAD
        q_ref[:, base:base + NOPE_D] = (qn * rstd * gn_ref[...]).astype(q_ref.dtype)
        roped = qr * cg_s[...] + qs * sg_s[...]
        q_ref[:, base + NOPE_D:base + HEAD_PAD] = (roped * rstd).astype(q_ref.dtype)


def _q_heads(cq, gl, w_uq3, gn, gr, gs, rc, rs, *, tm=512, heads=4):
    row = lambda i, h: (i, 0)
    vec = lambda i, h: (0, 0)
    return pl.pallas_call(
        functools.partial(_q_heads_kernel, heads=heads),
        grid=(TOKENS // tm, N_HEADS // heads),
        in_specs=[pl.BlockSpec((tm, Q_LORA), row),
                  pl.BlockSpec((1, Q_LORA), vec),
                  pl.BlockSpec((Q_LORA, heads * Q_HEAD_COLS), lambda i, h: (0, h)),
                  pl.BlockSpec((1, LANES), vec),
                  pl.BlockSpec((1, LANES), vec),
                  pl.BlockSpec((1, LANES), vec),
                  pl.BlockSpec((tm, LANES), row),
                  pl.BlockSpec((tm, LANES), row)],
        out_specs=pl.BlockSpec((tm, heads * HEAD_PAD), lambda i, h: (i, h)),
        out_shape=jax.ShapeDtypeStruct((TOKENS, N_HEADS * HEAD_PAD), BF16),
        scratch_shapes=[pltpu.VMEM((tm, Q_LORA), BF16),
                        pltpu.VMEM((tm, LANES), F32),
                        pltpu.VMEM((tm, LANES), F32)],
        compiler_params=_params("parallel", "arbitrary"),
        name="q_heads",
    )(cq, gl.reshape(1, Q_LORA), w_uq3, gn, gr, gs, rc, rs)


def _attn_kernel(q_ref, k_ref, v_ref, o_ref, m_s, acc_s, *, blk, heads):
    i = pl.program_id(2)
    m_s[...] = jnp.full_like(m_s, NEG_BIG)
    acc_s[...] = jnp.zeros_like(acc_s)

    def step(j, masked):
        start = pl.multiple_of(j * blk, blk)
        for g in range(heads):
            qk_cols = slice(g * HEAD_PAD, (g + 1) * HEAD_PAD)
            k = k_ref[pl.ds(start, blk), qk_cols]
            v = v_ref[pl.ds(start, blk), qk_cols]
            s = lax.dot_general(q_ref[:, qk_cols], k, (((1,), (1,)), ((), ())),
                                preferred_element_type=F32)
            if masked:
                row = lax.broadcasted_iota(jnp.int32, (blk, blk), 0)
                col = lax.broadcasted_iota(jnp.int32, (blk, blk), 1)
                s = jnp.where(col <= row, s, NEG_BIG)
            m_prev = m_s[g]
            m_new = jnp.maximum(m_prev, jnp.max(s, axis=1, keepdims=True))
            p = jnp.exp2(s - jnp.concatenate([m_new] * (blk // LANES), axis=1))
            alpha = jnp.exp2(m_prev - m_new)
            pv = jnp.dot(p.astype(BF16), v, preferred_element_type=F32)
            acc_s[g] = jnp.concatenate([alpha] * (HEAD_PAD // LANES), axis=1) * acc_s[g] + pv
            m_s[g] = m_new

    def body(j, carry):
        step(j, False)
        return carry

    lax.fori_loop(0, i, body, 0)
    step(i, True)
    for g in range(heads):
        acc = acc_s[g]
        o_ref[:, g * V_D:(g + 1) * V_D] = (acc[:, :V_D] / acc[:, V_D:]).astype(o_ref.dtype)


def _attention(q, k, v, *, blk=512, heads=4):
    nq = SEQ // blk
    return pl.pallas_call(
        functools.partial(_attn_kernel, blk=blk, heads=heads),
        grid=(BATCH, N_HEADS // heads, nq),
        in_specs=[pl.BlockSpec((blk, heads * HEAD_PAD), lambda b, h, i: (b * nq + i, h)),
                  pl.BlockSpec((SEQ, heads * HEAD_PAD), lambda b, h, i: (b, h)),
                  pl.BlockSpec((SEQ, heads * HEAD_PAD), lambda b, h, i: (b, h))],
        out_specs=pl.BlockSpec((blk, heads * V_D), lambda b, h, i: (b * nq + i, h)),
        out_shape=jax.ShapeDtypeStruct((TOKENS, N_HEADS * V_D), BF16),
        scratch_shapes=[pltpu.VMEM((heads, blk, LANES), F32),
                        pltpu.VMEM((heads, blk, HEAD_PAD), F32)],
        compiler_params=_params("parallel", "parallel", "arbitrary"),
        name="flash_attention",
    )(q, k, v)


def _swap_halves(t):
    half = ROPE_D // 2
    return jnp.concatenate([t[..., half:], t[..., :half]], axis=-1)


def _split_qk_gain(g, scale):
    pad = lambda v: jnp.pad(v * scale, (0, LANES - ROPE_D)).reshape(1, LANES)
    return (g[:NOPE_D] * scale).reshape(1, LANES), pad(g[NOPE_D:]), pad(_swap_halves(g[NOPE_D:]))


def kernel(x, positions, norm_mix, norm_ffn, lru_w_in, lru_b_in, lru_conv_w, lru_conv_b, lru_w_gate, lru_b_gate, lru_lambda, lru_w_out, lru_b_out, kv_norm_in, w_dkv, kv_latent_norm, w_ukv, k_norm, w_dq, q_latent_norm, w_uq, q_norm, w_o, ffn_w_in, ffn_w_out):
    h = x.reshape(TOKENS, D_MODEL)
    rc, rs = _rope_tables(positions)

    gate, rec_pre = _inproj(h, norm_mix[0], lru_w_in[0].astype(BF16), lru_b_in[0])
    y = _rglru(rec_pre, gate, lru_conv_w[0], lru_conv_b[0], lru_w_gate[0].astype(BF16),
               lru_b_gate[0], lru_lambda[0])
    h = _linear(y, lru_w_out[0].astype(BF16), bias=lru_b_out[0], resid=h, tn=D_MODEL,
                name="lru_out_proj")
    h = _ffn(h, norm_ffn[0], ffn_w_in[0].astype(BF16), ffn_w_out[0].astype(BF16))

    w_dkv_p = jnp.pad(w_dkv, ((0, 0), (0, LANES - ROPE_D))).astype(BF16)
    ckr, cq = _down_proj(h, kv_norm_in, norm_mix[1], w_dkv_p, w_dq[0].astype(BF16))
    kgn, kgr, _ = _split_qk_gain(k_norm, 1.0)
    k_sh, v_sh = _kv_heads(ckr, kv_latent_norm,
                           w_ukv.reshape(KV_LORA, N_HEADS * (NOPE_D + V_D)).astype(BF16),
                           kgn, kgr, rc, rs)

    zpad = jnp.zeros((Q_LORA, N_HEADS, LANES - ROPE_D), F32)
    w_rope = w_uq[0][..., NOPE_D:]
    w_uq3 = jnp.concatenate([w_uq[0][..., :NOPE_D], w_rope, zpad, _swap_halves(w_rope), zpad],
                            axis=-1)
    qgn, qgr, qgs = _split_qk_gain(q_norm[0], math.log2(math.e) / math.sqrt(QK_D))
    q = _q_heads(cq, q_latent_norm[0],
                 w_uq3.reshape(Q_LORA, N_HEADS * Q_HEAD_COLS).astype(BF16),
                 qgn, qgr, qgs, rc, rs)
    o = _attention(q, k_sh, v_sh)
    h = _linear(o, w_o[0].astype(BF16), resid=h, tn=D_MODEL, name="attn_out_proj")
    h = _ffn(h, norm_ffn[1], ffn_w_in[1].astype(BF16), ffn_w_out[1].astype(BF16))
    return h.reshape(BATCH, SEQ, D_MODEL)
```

```python
import functools
import math

import jax
import jax.numpy as jnp
from jax import lax
from jax.experimental import pallas as pl
from jax.experimental.pallas import tpu as pltpu

F32 = jnp.float32
BF16 = jnp.bfloat16

D_MODEL = 2048
BATCH = 4
SEQ = 4096
TOKENS = BATCH * SEQ
D_RNN = D_MODEL
LRU_BLOCK = 256
N_BLK = D_RNN // LRU_BLOCK
CONV_W = 4
LRU_C = 8.0
N_HEADS = 16
NOPE_D = 128
ROPE_D = 64
QK_D = NOPE_D + ROPE_D
V_D = 128
Q_LORA = 512
KV_LORA = 512
ROPE_THETA = 10000.0
D_FF = 5632
NORM_EPS = 1e-6

LANES = 128
SUBLANES = 8
HEAD_PAD = 2 * LANES
Q_HEAD_COLS = 3 * LANES
LANE_GROUPS = D_RNN // LANES
SCAN_GROUPS = 8
V7X_VMEM_BYTES = 64 * 1024 * 1024
VMEM_LIMIT = V7X_VMEM_BYTES * 7 // 8
NEG_BIG = -1e30


def _params(*semantics):
    return pltpu.CompilerParams(dimension_semantics=semantics, vmem_limit_bytes=VMEM_LIMIT)


def _rms(x, g):
    ms = jnp.mean(x * x, axis=-1, keepdims=True)
    return x * lax.rsqrt(ms + NORM_EPS) * g


def _rope(t, c, s):
    swapped = pltpu.roll(t, LANES - ROPE_D // 2, 1) + pltpu.roll(t, ROPE_D // 2, 1)
    return t * c + swapped * s


def _rope_table_kernel(pos_ref, invf_ref, cos_ref, sin_ref):
    ang = pos_ref[...].astype(F32) * invf_ref[...]
    cos_ref[...] = jnp.cos(ang)
    sin_ref[...] = jnp.sin(ang)


def _rope_tables(positions):
    half = ROPE_D // 2
    per_row = LANES // half
    rows = TOKENS // per_row
    inv_freq = ROPE_THETA ** (-jnp.arange(0, ROPE_D, 2, dtype=F32) / ROPE_D)
    pos = jnp.repeat(positions.reshape(rows, per_row), half, axis=1)
    invf = jnp.tile(inv_freq, per_row).reshape(1, LANES)
    tr = 512
    cosp, sinp = pl.pallas_call(
        _rope_table_kernel,
        grid=(rows // tr,),
        in_specs=[pl.BlockSpec((tr, LANES), lambda i: (i, 0)),
                  pl.BlockSpec((1, LANES), lambda i: (0, 0))],
        out_specs=[pl.BlockSpec((tr, LANES), lambda i: (i, 0)),
                   pl.BlockSpec((tr, LANES), lambda i: (i, 0))],
        out_shape=[jax.ShapeDtypeStruct((rows, LANES), F32)] * 2,
        compiler_params=_params("arbitrary"),
        name="rope_tables",
    )(pos, invf)
    cos = cosp.reshape(TOKENS, half)
    sin = sinp.reshape(TOKENS, half)
    z = jnp.zeros((TOKENS, LANES - ROPE_D), F32)
    rc = jnp.concatenate([cos, cos, z], axis=1)
    rs = jnp.concatenate([-sin, sin, z], axis=1)
    return rc, rs


def _linear_kernel(*refs, has_gain, has_bias, has_resid, stage_x):
    refs = list(refs)
    x_ref = refs.pop(0)
    w_ref = refs.pop(0)
    g_ref = refs.pop(0) if has_gain else None
    b_ref = refs.pop(0) if has_bias else None
    r_ref = refs.pop(0) if has_resid else None
    o_ref = refs.pop(0)
    if stage_x:
        xs_ref = refs.pop(0)

        @pl.when(pl.program_id(1) == 0)
        def _():
            x = x_ref[...].astype(F32)
            if has_gain:
                x = _rms(x, g_ref[...])
            xs_ref[...] = x.astype(BF16)

        xv = xs_ref[...]
    else:
        xv = x_ref[...]
    acc = jnp.dot(xv, w_ref[...], preferred_element_type=F32)
    if has_bias:
        acc = acc + b_ref[...]
    if has_resid:
        acc = acc + r_ref[...]
    o_ref[...] = acc.astype(o_ref.dtype)


def _linear(x, w, *, gain=None, bias=None, resid=None, out_dtype=F32, tm=512, tn=1024, name):
    m, k = x.shape
    n = w.shape[1]
    tn = min(tn, n)
    stage_x = gain is not None or x.dtype != BF16
    in_specs = [pl.BlockSpec((tm, k), lambda i, j: (i, 0)),
                pl.BlockSpec((k, tn), lambda i, j: (0, j))]
    args = [x, w]
    if gain is not None:
        in_specs.append(pl.BlockSpec((1, k), lambda i, j: (0, 0)))
        args.append(gain.reshape(1, k))
    if bias is not None:
        in_specs.append(pl.BlockSpec((1, tn), lambda i, j: (0, j)))
        args.append(bias.reshape(1, n))
    if resid is not None:
        in_specs.append(pl.BlockSpec((tm, tn), lambda i, j: (i, j)))
        args.append(resid)
    scratch = [pltpu.VMEM((tm, k), BF16)] if stage_x else []
    return pl.pallas_call(
        functools.partial(_linear_kernel, has_gain=gain is not None, has_bias=bias is not None,
                          has_resid=resid is not None, stage_x=stage_x),
        grid=(m // tm, n // tn),
        in_specs=in_specs,
        out_specs=pl.BlockSpec((tm, tn), lambda i, j: (i, j)),
        out_shape=jax.ShapeDtypeStruct((m, n), out_dtype),
        scratch_shapes=scratch,
        compiler_params=_params("parallel", "arbitrary"),
        name=name,
    )(*args)


def _down_proj_kernel(x_ref, gkv_ref, gq_ref, wkv_ref, wq_ref, ckr_ref, cq_ref):
    x = x_ref[...]
    xhat = x * lax.rsqrt(jnp.mean(x * x, axis=-1, keepdims=True) + NORM_EPS)
    ckr_ref[...] = jnp.dot((xhat * gkv_ref[...]).astype(BF16), wkv_ref[...],
                           preferred_element_type=F32)
    cq_ref[...] = jnp.dot((xhat * gq_ref[...]).astype(BF16), wq_ref[...],
                          preferred_element_type=F32)


def _down_proj(x, g_kv, g_q, w_dkv, w_dq, *, tm=1024):
    n_kv = w_dkv.shape[1]
    row = lambda i: (i, 0)
    full = lambda i: (0, 0)
    return pl.pallas_call(
        _down_proj_kernel,
        grid=(TOKENS // tm,),
        in_specs=[pl.BlockSpec((tm, D_MODEL), row),
                  pl.BlockSpec((1, D_MODEL), full),
                  pl.BlockSpec((1, D_MODEL), full),
                  pl.BlockSpec((D_MODEL, n_kv), full),
                  pl.BlockSpec((D_MODEL, Q_LORA), full)],
        out_specs=[pl.BlockSpec((tm, n_kv), row),
                   pl.BlockSpec((tm, Q_LORA), row)],
        out_shape=[jax.ShapeDtypeStruct((TOKENS, n_kv), F32),
                   jax.ShapeDtypeStruct((TOKENS, Q_LORA), F32)],
        compiler_params=_params("parallel"),
        name="mla_down_proj",
    )(x, g_kv.reshape(1, D_MODEL), g_q.reshape(1, D_MODEL), w_dkv, w_dq)


def _inproj_kernel(x_ref, g_ref, wa_ref, wb_ref, ba_ref, bb_ref, gate_ref, rec_ref, xs_ref):
    @pl.when(pl.program_id(1) == 0)
    def _():
        xs_ref[...] = _rms(x_ref[...], g_ref[...]).astype(BF16)

    xv = xs_ref[...]
    ga = jnp.dot(xv, wa_ref[...], preferred_element_type=F32) + ba_ref[...]
    gate_ref[...] = jax.nn.gelu(ga).astype(gate_ref.dtype)
    rec_ref[...] = jnp.dot(xv, wb_ref[...], preferred_element_type=F32) + bb_ref[...]


def _inproj(x, gain, w_in, b_in, *, tm=512, tn=1024):
    nj = D_RNN // tn
    b2 = b_in.reshape(1, 2 * D_RNN)
    return pl.pallas_call(
        _inproj_kernel,
        grid=(TOKENS // tm, nj),
        in_specs=[pl.BlockSpec((tm, D_MODEL), lambda i, j: (i, 0)),
                  pl.BlockSpec((1, D_MODEL), lambda i, j: (0, 0)),
                  pl.BlockSpec((D_MODEL, tn), lambda i, j: (0, j)),
                  pl.BlockSpec((D_MODEL, tn), lambda i, j: (0, j + nj)),
                  pl.BlockSpec((1, tn), lambda i, j: (0, j)),
                  pl.BlockSpec((1, tn), lambda i, j: (0, j + nj))],
        out_specs=[pl.BlockSpec((tm, tn), lambda i, j: (i, j)),
                   pl.BlockSpec((tm, tn), lambda i, j: (i, j))],
        out_shape=[jax.ShapeDtypeStruct((TOKENS, D_RNN), BF16),
                   jax.ShapeDtypeStruct((TOKENS, D_RNN), F32)],
        scratch_shapes=[pltpu.VMEM((tm, D_MODEL), BF16)],
        compiler_params=_params("parallel", "arbitrary"),
        name="lru_in_proj",
    )(x, gain.reshape(1, D_MODEL), w_in, w_in, b2, b2)


def _rglru_kernel(x_ref, gate_ref, cw_ref, cb_ref, wg_ref, bg_ref, lam_ref, y_ref,
                  xp_s, a_s, b_s, tail_s, eq_s, cm_s, hc_s, *, chunk):
    c = pl.program_id(1)
    seg = chunk // SUBLANES
    halo = SUBLANES * (CONV_W - 1)

    @pl.when(c == 0)
    def _():
        tail_s[...] = jnp.zeros_like(tail_s)
        hc_s[...] = jnp.zeros_like(hc_s)

    for g in range(LANE_GROUPS):
        lanes = slice(g * LANES, (g + 1) * LANES)
        for j in range(SUBLANES):
            xp_s[g, pl.ds(halo + j, seg, stride=SUBLANES), :] = x_ref[j * seg:(j + 1) * seg, lanes]

    first_sublane = lax.broadcasted_iota(jnp.int32, (SUBLANES, LANES), 0) == 0
    for g in range(LANE_GROUPS):
        for d in range(1, CONV_W):
            slot = slice(SUBLANES * (CONV_W - 1 - d), SUBLANES * (CONV_W - d))
            cur = xp_s[g, halo + SUBLANES * (seg - d):halo + SUBLANES * (seg - d + 1), :]
            prev = tail_s[g, slot, :]
            xp_s[g, slot, :] = jnp.where(first_sublane, pltpu.roll(prev, 1, 0),
                                         pltpu.roll(cur, 1, 0))
            tail_s[g, slot, :] = cur

    lam = lam_ref[...]
    sp = jnp.maximum(-lam, 0.0) + jnp.log1p(jnp.exp(-jnp.abs(lam)))
    neg_half_c_sp = (-0.5 * LRU_C) * sp
    per_blk = LRU_BLOCK // LANES
    for n in range(N_BLK):
        parts = []
        for sub in range(per_blk):
            g = n * per_blk + sub
            lanes = slice(g * LANES, (g + 1) * LANES)
            acc = cb_ref[:, lanes] + cw_ref[CONV_W - 1:CONV_W, lanes] * xp_s[g, halo:halo + chunk, :]
            for d in range(1, CONV_W):
                off = halo - SUBLANES * d
                acc = acc + cw_ref[CONV_W - 1 - d:CONV_W - d, lanes] * xp_s[g, off:off + chunk, :]
            parts.append(acc)
        xn = jnp.concatenate(parts, axis=1)
        cols = slice(n * LRU_BLOCK, (n + 1) * LRU_BLOCK)
        gh = jnp.dot(xn.astype(BF16), wg_ref[n], preferred_element_type=F32) + bg_ref[n:n + 1, :]
        tr = jnp.tanh(gh[:, :LRU_BLOCK])
        i = 0.5 * jnp.tanh(gh[:, LRU_BLOCK:]) + 0.5
        log_a = (tr + 1.0) * neg_half_c_sp[:, cols]
        a = jnp.exp(log_a)
        x1 = -jnp.tanh(log_a) * (a * a + 1.0)
        mult = jnp.where(x1 > 0.0, x1 * lax.rsqrt(x1), 0.0)
        b = mult * (i * xn)
        for sub in range(per_blk):
            lanes = slice(sub * LANES, (sub + 1) * LANES)
            a_s[n * per_blk + sub] = a[:, lanes]
            b_s[n * per_blk + sub] = b[:, lanes]

    def tile(ref, g, s):
        return ref[g, pl.ds(pl.multiple_of(s * SUBLANES, SUBLANES), SUBLANES), :]

    for g0 in range(0, LANE_GROUPS, SCAN_GROUPS):
        groups = range(g0, g0 + SCAN_GROUPS)

        def pass1(s, carry):
            hs, ps = carry
            a = [tile(a_s, g, s) for g in groups]
            b = [tile(b_s, g, s) for g in groups]
            return (tuple(ai * hi + bi for ai, hi, bi in zip(a, hs, b)),
                    tuple(ai * pi for ai, pi in zip(a, ps)))

        zero = jnp.zeros((SUBLANES, LANES), F32)
        one = jnp.ones((SUBLANES, LANES), F32)
        e, q = lax.fori_loop(0, seg, pass1, ((zero,) * SCAN_GROUPS, (one,) * SCAN_GROUPS),
                             unroll=8)
        for idx, g in enumerate(groups):
            eq_s[0] = e[idx]
            eq_s[1] = q[idx]
            h = hc_s[g, 0:1, :]
            for j in range(SUBLANES):
                cm_s[g, j:j + 1, :] = h
                h = eq_s[1, j:j + 1, :] * h + eq_s[0, j:j + 1, :]
            hc_s[g, 0:1, :] = h

        def pass2(s, hs):
            out = []
            for g, h in zip(groups, hs):
                h = tile(a_s, g, s) * h + tile(b_s, g, s)
                b_s[g, pl.ds(pl.multiple_of(s * SUBLANES, SUBLANES), SUBLANES), :] = h
                out.append(h)
            return tuple(out)

        lax.fori_loop(0, seg, pass2, tuple(cm_s[g] for g in groups), unroll=8)

    for g in range(LANE_GROUPS):
        lanes = slice(g * LANES, (g + 1) * LANES)
        for j in range(SUBLANES):
            rows = slice(j * seg, (j + 1) * seg)
            h = b_s[g, pl.ds(j, seg, stride=SUBLANES), :]
            y_ref[rows, lanes] = (gate_ref[rows, lanes].astype(F32) * h).astype(y_ref.dtype)


def _rglru(rec_pre, gate, conv_w, conv_b, w_gate, b_gate, lam, *, chunk=512):
    nc = SEQ // chunk
    row = lambda b, c: (b * nc + c, 0)
    full2 = lambda b, c: (0, 0)
    return pl.pallas_call(
        functools.partial(_rglru_kernel, chunk=chunk),
        grid=(BATCH, nc),
        in_specs=[pl.BlockSpec((chunk, D_RNN), row),
                  pl.BlockSpec((chunk, D_RNN), row),
                  pl.BlockSpec((CONV_W, D_RNN), full2),
                  pl.BlockSpec((1, D_RNN), full2),
                  pl.BlockSpec((N_BLK, LRU_BLOCK, 2 * LRU_BLOCK), lambda b, c: (0, 0, 0)),
                  pl.BlockSpec((N_BLK, 2 * LRU_BLOCK), full2),
                  pl.BlockSpec((1, D_RNN), full2)],
        out_specs=pl.BlockSpec((chunk, D_RNN), row),
        out_shape=jax.ShapeDtypeStruct((TOKENS, D_RNN), BF16),
        scratch_shapes=[pltpu.VMEM((LANE_GROUPS, SUBLANES * (CONV_W - 1) + chunk, LANES), F32),
                        pltpu.VMEM((LANE_GROUPS, chunk, LANES), F32),
                        pltpu.VMEM((LANE_GROUPS, chunk, LANES), F32),
                        pltpu.VMEM((LANE_GROUPS, SUBLANES * (CONV_W - 1), LANES), F32),
                        pltpu.VMEM((2, SUBLANES, LANES), F32),
                        pltpu.VMEM((LANE_GROUPS, SUBLANES, LANES), F32),
                        pltpu.VMEM((LANE_GROUPS, SUBLANES, LANES), F32)],
        compiler_params=_params("parallel", "arbitrary"),
        name="rglru_core",
    )(rec_pre, gate, conv_w, conv_b.reshape(1, D_RNN), w_gate, b_gate, lam.reshape(1, D_RNN))


def _ffn_kernel(h_ref, g_ref, wa_ref, wb_ref, w2_ref, o_ref, hn_s):
    f = pl.program_id(1)

    @pl.when(f == 0)
    def _():
        h = h_ref[...]
        hn_s[...] = _rms(h, g_ref[...]).astype(BF16)
        o_ref[...] = h

    hn = hn_s[...]
    ua = jnp.dot(hn, wa_ref[...], preferred_element_type=F32)
    ub = jnp.dot(hn, wb_ref[...], preferred_element_type=F32)
    act = (jax.nn.silu(ua) * ub).astype(BF16)
    o_ref[...] += jnp.dot(act, w2_ref[...], preferred_element_type=F32)


def _ffn(h, gain, w_in, w_out, *, tm=1024, tf=512):
    nf = D_FF // tf
    return pl.pallas_call(
        _ffn_kernel,
        grid=(TOKENS // tm, nf),
        in_specs=[pl.BlockSpec((tm, D_MODEL), lambda i, f: (i, 0)),
                  pl.BlockSpec((1, D_MODEL), lambda i, f: (0, 0)),
                  pl.BlockSpec((D_MODEL, tf), lambda i, f: (0, f)),
                  pl.BlockSpec((D_MODEL, tf), lambda i, f: (0, f + nf)),
                  pl.BlockSpec((tf, D_MODEL), lambda i, f: (f, 0))],
        out_specs=pl.BlockSpec((tm, D_MODEL), lambda i, f: (i, 0)),
        out_shape=jax.ShapeDtypeStruct((TOKENS, D_MODEL), F32),
        scratch_shapes=[pltpu.VMEM((tm, D_MODEL), BF16)],
        compiler_params=_params("parallel", "arbitrary"),
        name="swiglu_ffn",
    )(h, gain.reshape(1, D_MODEL), w_in, w_in, w_out)


def _kv_heads_kernel(c_ref, kr_ref, gl_ref, w_ref, gn_ref, gr_ref, rc_ref, rs_ref,
                     k_ref, v_ref, cn_s, krr_s, ssr_s, *, heads):
    @pl.when(pl.program_id(1) == 0)
    def _():
        cn_s[...] = _rms(c_ref[...], gl_ref[...]).astype(BF16)
        kr = kr_ref[...]
        ssr_s[...] = jnp.broadcast_to(jnp.sum(kr * kr, axis=1, keepdims=True), ssr_s.shape)
        krr_s[...] = _rope(kr * gr_ref[...], rc_ref[...], rs_ref[...])

    ones = jnp.ones((v_ref.shape[0], HEAD_PAD - V_D), v_ref.dtype)
    kv_all = jnp.dot(cn_s[...], w_ref[...], preferred_element_type=F32)
    for g in range(heads):
        base = g * HEAD_PAD
        kv = kv_all[:, base:base + HEAD_PAD]
        kn = kv[:, :NOPE_D]
        ss = jnp.sum(kn * kn, axis=1, keepdims=True) + ssr_s[...]
        rstd = lax.rsqrt(ss * (1.0 / QK_D) + NORM_EPS)
        k_ref[:, base:base + NOPE_D] = (kn * rstd * gn_ref[...]).astype(k_ref.dtype)
        k_ref[:, base + NOPE_D:base + HEAD_PAD] = (krr_s[...] * rstd).astype(k_ref.dtype)
        v_ref[:, base:base + V_D] = kv[:, NOPE_D:].astype(v_ref.dtype)
        v_ref[:, base + V_D:base + HEAD_PAD] = ones


def _kv_heads(ckr, gl, w_ukv, gn, gr, rc, rs, *, tm=1024, heads=4):
    rope_blk = KV_LORA // LANES
    row = lambda i, h: (i, 0)
    vec = lambda i, h: (0, 0)
    return pl.pallas_call(
        functools.partial(_kv_heads_kernel, heads=heads),
        grid=(TOKENS // tm, N_HEADS // heads),
        in_specs=[pl.BlockSpec((tm, KV_LORA), row),
                  pl.BlockSpec((tm, LANES), lambda i, h: (i, rope_blk)),
                  pl.BlockSpec((1, KV_LORA), vec),
                  pl.BlockSpec((KV_LORA, heads * HEAD_PAD), lambda i, h: (0, h)),
                  pl.BlockSpec((1, LANES), vec),
                  pl.BlockSpec((1, LANES), vec),
                  pl.BlockSpec((tm, LANES), row),
                  pl.BlockSpec((tm, LANES), row)],
        out_specs=[pl.BlockSpec((tm, heads * HEAD_PAD), lambda i, h: (i, h)),
                   pl.BlockSpec((tm, heads * HEAD_PAD), lambda i, h: (i, h))],
        out_shape=[jax.ShapeDtypeStruct((TOKENS, N_HEADS * HEAD_PAD), BF16),
                   jax.ShapeDtypeStruct((TOKENS, N_HEADS * HEAD_PAD), BF16)],
        scratch_shapes=[pltpu.VMEM((tm, KV_LORA), BF16),
                        pltpu.VMEM((tm, LANES), F32),
                        pltpu.VMEM((tm, LANES), F32)],
        compiler_params=_params("parallel", "arbitrary"),
        name="kv_heads",
    )(ckr, ckr, gl.reshape(1, KV_LORA), w_ukv, gn, gr, rc, rs)


def _q_heads_kernel(c_ref, gl_ref, w_ref, gn_ref, gr_ref, gs_ref, rc_ref, rs_ref,
                    q_ref, cn_s, cg_s, sg_s, *, heads):
    @pl.when(pl.program_id(1) == 0)
    def _():
        cn_s[...] = _rms(c_ref[...], gl_ref[...]).astype(BF16)
        cg_s[...] = rc_ref[...] * gr_ref[...]
        sg_s[...] = rs_ref[...] * gs_ref[...]

    q_all = jnp.dot(cn_s[...], w_ref[...], preferred_element_type=F32)
    for g in range(heads):
        q = q_all[:, g * Q_HEAD_COLS:(g + 1) * Q_HEAD_COLS]
        qn = q[:, :NOPE_D]
        qr = q[:, NOPE_D:NOPE_D + LANES]
        qs = q[:, NOPE_D + LANES:]
        ss = jnp.sum(qn * qn + qr * qr, axis=1, keepdims=True)
        rstd = lax.rsqrt(ss * (1.0 / QK_D) + NORM_EPS)
        base = g * HEAD_PAD
        q_ref[:, base:base + NOPE_D] = (qn * rstd * gn_ref[...]).astype(q_ref.dtype)
        roped = qr * cg_s[...] + qs * sg_s[...]
        q_ref[:, base + NOPE_D:base + HEAD_PAD] = (roped * rstd).astype(q_ref.dtype)


def _q_heads(cq, gl, w_uq3, gn, gr, gs, rc, rs, *, tm=1024, heads=4):
    row = lambda i, h: (i, 0)
    vec = lambda i, h: (0, 0)
    return pl.pallas_call(
        functools.partial(_q_heads_kernel, heads=heads),
        grid=(TOKENS // tm, N_HEADS // heads),
        in_specs=[pl.BlockSpec((tm, Q_LORA), row),
                  pl.BlockSpec((1, Q_LORA), vec),
                  pl.BlockSpec((Q_LORA, heads * Q_HEAD_COLS), lambda i, h: (0, h)),
                  pl.BlockSpec((1, LANES), vec),
                  pl.BlockSpec((1, LANES), vec),
                  pl.BlockSpec((1, LANES), vec),
                  pl.BlockSpec((tm, LANES), row),
                  pl.BlockSpec((tm, LANES), row)],
        out_specs=pl.BlockSpec((tm, heads * HEAD_PAD), lambda i, h: (i, h)),
        out_shape=jax.ShapeDtypeStruct((TOKENS, N_HEADS * HEAD_PAD), BF16),
        scratch_shapes=[pltpu.VMEM((tm, Q_LORA), BF16),
                        pltpu.VMEM((tm, LANES), F32),
                        pltpu.VMEM((tm, LANES), F32)],
        compiler_params=_params("parallel", "arbitrary"),
        name="q_heads",
    )(cq, gl.reshape(1, Q_LORA), w_uq3, gn, gr, gs, rc, rs)


def _attn_kernel(q_ref, k_ref, v_ref, o_ref, m_s, acc_s, *, blk, heads, subs):
    i = pl.program_id(2)
    m_s[...] = jnp.full_like(m_s, NEG_BIG)
    acc_s[...] = jnp.zeros_like(acc_s)

    def chain(g, u, start, masked):
        cols = slice(g * HEAD_PAD, (g + 1) * HEAD_PAD)
        rows = slice(u * blk, (u + 1) * blk)
        k = k_ref[pl.ds(start, blk), cols]
        v = v_ref[pl.ds(start, blk), cols]
        s = lax.dot_general(q_ref[rows, cols], k, (((1,), (1,)), ((), ())),
                            preferred_element_type=F32)
        if masked:
            row = lax.broadcasted_iota(jnp.int32, (blk, blk), 0)
            col = lax.broadcasted_iota(jnp.int32, (blk, blk), 1)
            s = jnp.where(col <= row, s, NEG_BIG)
        m_prev = m_s[g, rows, :]
        m_new = jnp.maximum(m_prev, jnp.max(s, axis=1, keepdims=True))
        p = jnp.exp2(s - jnp.concatenate([m_new] * (blk // LANES), axis=1))
        alpha = jnp.exp2(m_prev - m_new)
        pv = jnp.dot(p.astype(BF16), v, preferred_element_type=F32)
        acc_s[g, rows, :] = (jnp.concatenate([alpha] * (HEAD_PAD // LANES), axis=1)
                             * acc_s[g, rows, :] + pv)
        m_s[g, rows, :] = m_new

    def step(j, active):
        start = pl.multiple_of(j * blk, blk)
        for g in range(heads):
            for u, masked in active:
                chain(g, u, start, masked)

    def body(j, carry):
        step(j, [(u, False) for u in range(subs)])
        return carry

    lax.fori_loop(0, subs * i, body, 0)
    for d in range(subs):
        step(subs * i + d, [(d, True)] + [(u, False) for u in range(d + 1, subs)])
    for g in range(heads):
        acc = acc_s[g]
        o_ref[:, g * V_D:(g + 1) * V_D] = (acc[:, :V_D] / acc[:, V_D:]).astype(o_ref.dtype)


def _attention(q, k, v, *, blk=512, heads=4, subs=2):
    tq = subs * blk
    nq = SEQ // tq
    return pl.pallas_call(
        functools.partial(_attn_kernel, blk=blk, heads=heads, subs=subs),
        grid=(BATCH, N_HEADS // heads, nq),
        in_specs=[pl.BlockSpec((tq, heads * HEAD_PAD), lambda b, h, i: (b * nq + i, h)),
                  pl.BlockSpec((SEQ, heads * HEAD_PAD), lambda b, h, i: (b, h)),
                  pl.BlockSpec((SEQ, heads * HEAD_PAD), lambda b, h, i: (b, h))],
        out_specs=pl.BlockSpec((tq, heads * V_D), lambda b, h, i: (b * nq + i, h)),
        out_shape=jax.ShapeDtypeStruct((TOKENS, N_HEADS * V_D), BF16),
        scratch_shapes=[pltpu.VMEM((heads, tq, LANES), F32),
                        pltpu.VMEM((heads, tq, HEAD_PAD), F32)],
        compiler_params=_params("parallel", "parallel", "arbitrary"),
        name="flash_attention",
    )(q, k, v)


def _swap_halves(t):
    half = ROPE_D // 2
    return jnp.concatenate([t[..., half:], t[..., :half]], axis=-1)


def _split_qk_gain(g, scale):
    pad = lambda v: jnp.pad(v * scale, (0, LANES - ROPE_D)).reshape(1, LANES)
    return (g[:NOPE_D] * scale).reshape(1, LANES), pad(g[NOPE_D:]), pad(_swap_halves(g[NOPE_D:]))


def kernel(x, positions, norm_mix, norm_ffn, lru_w_in, lru_b_in, lru_conv_w, lru_conv_b, lru_w_gate, lru_b_gate, lru_lambda, lru_w_out, lru_b_out, kv_norm_in, w_dkv, kv_latent_norm, w_ukv, k_norm, w_dq, q_latent_norm, w_uq, q_norm, w_o, ffn_w_in, ffn_w_out):
    h = x.reshape(TOKENS, D_MODEL)
    rc, rs = _rope_tables(positions)

    gate, rec_pre = _inproj(h, norm_mix[0], lru_w_in[0].astype(BF16), lru_b_in[0])
    y = _rglru(rec_pre, gate, lru_conv_w[0], lru_conv_b[0], (0.5 * lru_w_gate[0]).astype(BF16),
               0.5 * lru_b_gate[0], lru_lambda[0])
    h = _linear(y, lru_w_out[0].astype(BF16), bias=lru_b_out[0], resid=h, tn=D_MODEL,
                name="lru_out_proj")
    h = _ffn(h, norm_ffn[0], ffn_w_in[0].astype(BF16), ffn_w_out[0].astype(BF16))

    w_dkv_p = jnp.pad(w_dkv, ((0, 0), (0, LANES - ROPE_D))).astype(BF16)
    ckr, cq = _down_proj(h, kv_norm_in, norm_mix[1], w_dkv_p, w_dq[0].astype(BF16))
    kgn, kgr, _ = _split_qk_gain(k_norm, 1.0)
    k_sh, v_sh = _kv_heads(ckr, kv_latent_norm,
                           w_ukv.reshape(KV_LORA, N_HEADS * (NOPE_D + V_D)).astype(BF16),
                           kgn, kgr, rc, rs)

    zpad = jnp.zeros((Q_LORA, N_HEADS, LANES - ROPE_D), F32)
    w_rope = w_uq[0][..., NOPE_D:]
    w_uq3 = jnp.concatenate([w_uq[0][..., :NOPE_D], w_rope, zpad, _swap_halves(w_rope), zpad],
                            axis=-1)
    qgn, qgr, qgs = _split_qk_gain(q_norm[0], math.log2(math.e) / math.sqrt(QK_D))
    q = _q_heads(cq, q_latent_norm[0],
                 w_uq3.reshape(Q_LORA, N_HEADS * Q_HEAD_COLS).astype(BF16),
                 qgn, qgr, qgs, rc, rs)
    o = _attention(q, k_sh, v_sh)
    h = _linear(o, w_o[0].astype(BF16), resid=h, tn=D_MODEL, name="attn_out_proj")
    h = _ffn(h, norm_ffn[1], ffn_w_in[1].astype(BF16), ffn_w_out[1].astype(BF16))
    return h.reshape(BATCH, SEQ, D_MODEL)
```

```python
import functools
import math

import jax
import jax.numpy as jnp
from jax import lax
from jax.experimental import pallas as pl
from jax.experimental.pallas import tpu as pltpu

F32 = jnp.float32
BF16 = jnp.bfloat16

D_MODEL = 2048
BATCH = 4
SEQ = 4096
TOKENS = BATCH * SEQ
D_RNN = D_MODEL
LRU_BLOCK = 256
N_BLK = D_RNN // LRU_BLOCK
CONV_W = 4
LRU_C = 8.0
N_HEADS = 16
NOPE_D = 128
ROPE_D = 64
QK_D = NOPE_D + ROPE_D
V_D = 128
Q_LORA = 512
KV_LORA = 512
ROPE_THETA = 10000.0
D_FF = 5632
NORM_EPS = 1e-6

LANES = 128
SUBLANES = 8
HEAD_PAD = 2 * LANES
DOWN_PROJ_CHUNKS = 2
Q_HEAD_COLS = 3 * LANES
LANE_GROUPS = D_RNN // LANES
SCAN_GROUPS = 8
V7X_VMEM_BYTES = 64 * 1024 * 1024
VMEM_LIMIT = V7X_VMEM_BYTES * 7 // 8
NEG_BIG = -1e30


def _params(*semantics):
    return pltpu.CompilerParams(dimension_semantics=semantics, vmem_limit_bytes=VMEM_LIMIT)


def _rms(x, g):
    ms = jnp.mean(x * x, axis=-1, keepdims=True)
    return x * lax.rsqrt(ms + NORM_EPS) * g


def _rope(t, c, s):
    swapped = pltpu.roll(t, LANES - ROPE_D // 2, 1) + pltpu.roll(t, ROPE_D // 2, 1)
    return t * c + swapped * s


def _rope_table_kernel(pos_ref, invf_ref, cos_ref, sin_ref):
    ang = pos_ref[...].astype(F32) * invf_ref[...]
    cos_ref[...] = jnp.cos(ang)
    sin_ref[...] = jnp.sin(ang)


def _rope_tables(positions):
    half = ROPE_D // 2
    per_row = LANES // half
    rows = TOKENS // per_row
    inv_freq = ROPE_THETA ** (-jnp.arange(0, ROPE_D, 2, dtype=F32) / ROPE_D)
    pos = jnp.broadcast_to(positions.reshape(rows, per_row, 1), (rows, per_row, half))
    pos = pos.reshape(rows, LANES)
    invf = jnp.tile(inv_freq, per_row).reshape(1, LANES)
    tr = 512
    cosp, sinp = pl.pallas_call(
        _rope_table_kernel,
        grid=(rows // tr,),
        in_specs=[pl.BlockSpec((tr, LANES), lambda i: (i, 0)),
                  pl.BlockSpec((1, LANES), lambda i: (0, 0))],
        out_specs=[pl.BlockSpec((tr, LANES), lambda i: (i, 0)),
                   pl.BlockSpec((tr, LANES), lambda i: (i, 0))],
        out_shape=[jax.ShapeDtypeStruct((rows, LANES), F32)] * 2,
        compiler_params=_params("arbitrary"),
        name="rope_tables",
    )(pos, invf)
    cos = cosp.reshape(TOKENS, half)
    sin = sinp.reshape(TOKENS, half)
    z = jnp.zeros((TOKENS, LANES - ROPE_D), F32)
    rc = jnp.concatenate([cos, cos, z], axis=1)
    rs = jnp.concatenate([-sin, sin, z], axis=1)
    return rc, rs


def _linear_kernel(*refs, has_gain, has_bias, has_resid, stage_x):
    refs = list(refs)
    x_ref = refs.pop(0)
    w_ref = refs.pop(0)
    g_ref = refs.pop(0) if has_gain else None
    b_ref = refs.pop(0) if has_bias else None
    r_ref = refs.pop(0) if has_resid else None
    o_ref = refs.pop(0)
    if stage_x:
        xs_ref = refs.pop(0)

        @pl.when(pl.program_id(1) == 0)
        def _():
            x = x_ref[...].astype(F32)
            if has_gain:
                x = _rms(x, g_ref[...])
            xs_ref[...] = x.astype(BF16)

        xv = xs_ref[...]
    else:
        xv = x_ref[...]
    acc = jnp.dot(xv, w_ref[...], preferred_element_type=F32)
    if has_bias:
        acc = acc + b_ref[...]
    if has_resid:
        acc = acc + r_ref[...]
    o_ref[...] = acc.astype(o_ref.dtype)


def _linear(x, w, *, gain=None, bias=None, resid=None, out_dtype=F32, tm=512, tn=1024, name):
    m, k = x.shape
    n = w.shape[1]
    tn = min(tn, n)
    stage_x = gain is not None or x.dtype != BF16
    in_specs = [pl.BlockSpec((tm, k), lambda i, j: (i, 0)),
                pl.BlockSpec((k, tn), lambda i, j: (0, j))]
    args = [x, w]
    if gain is not None:
        in_specs.append(pl.BlockSpec((1, k), lambda i, j: (0, 0)))
        args.append(gain.reshape(1, k))
    if bias is not None:
        in_specs.append(pl.BlockSpec((1, tn), lambda i, j: (0, j)))
        args.append(bias.reshape(1, n))
    if resid is not None:
        in_specs.append(pl.BlockSpec((tm, tn), lambda i, j: (i, j)))
        args.append(resid)
    scratch = [pltpu.VMEM((tm, k), BF16)] if stage_x else []
    return pl.pallas_call(
        functools.partial(_linear_kernel, has_gain=gain is not None, has_bias=bias is not None,
                          has_resid=resid is not None, stage_x=stage_x),
        grid=(m // tm, n // tn),
        in_specs=in_specs,
        out_specs=pl.BlockSpec((tm, tn), lambda i, j: (i, j)),
        out_shape=jax.ShapeDtypeStruct((m, n), out_dtype),
        scratch_shapes=scratch,
        compiler_params=_params("parallel", "arbitrary"),
        name=name,
    )(*args)


def _down_proj_kernel(x_ref, gkv_ref, gq_ref, wkv_ref, wq_ref, ckr_ref, cq_ref):
    chunk = x_ref.shape[0] // DOWN_PROJ_CHUNKS
    for r in range(DOWN_PROJ_CHUNKS):
        rows = slice(r * chunk, (r + 1) * chunk)
        x = x_ref[rows, :]
        xhat = x * lax.rsqrt(jnp.mean(x * x, axis=-1, keepdims=True) + NORM_EPS)
        ckr_ref[rows, :] = jnp.dot((xhat * gkv_ref[...]).astype(BF16), wkv_ref[...],
                                   preferred_element_type=F32)
        cq_ref[rows, :] = jnp.dot((xhat * gq_ref[...]).astype(BF16), wq_ref[...],
                                  preferred_element_type=F32)


def _down_proj(x, g_kv, g_q, w_dkv, w_dq, *, tm=1024):
    n_kv = w_dkv.shape[1]
    row = lambda i: (i, 0)
    full = lambda i: (0, 0)
    return pl.pallas_call(
        _down_proj_kernel,
        grid=(TOKENS // tm,),
        in_specs=[pl.BlockSpec((tm, D_MODEL), row),
                  pl.BlockSpec((1, D_MODEL), full),
                  pl.BlockSpec((1, D_MODEL), full),
                  pl.BlockSpec((D_MODEL, n_kv), full),
                  pl.BlockSpec((D_MODEL, Q_LORA), full)],
        out_specs=[pl.BlockSpec((tm, n_kv), row),
                   pl.BlockSpec((tm, Q_LORA), row)],
        out_shape=[jax.ShapeDtypeStruct((TOKENS, n_kv), F32),
                   jax.ShapeDtypeStruct((TOKENS, Q_LORA), F32)],
        compiler_params=_params("parallel"),
        name="mla_down_proj",
    )(x, g_kv.reshape(1, D_MODEL), g_q.reshape(1, D_MODEL), w_dkv, w_dq)


def _inproj_kernel(x_ref, g_ref, wa_ref, wb_ref, ba_ref, bb_ref, gate_ref, rec_ref, xs_ref):
    @pl.when(pl.program_id(1) == 0)
    def _():
        xs_ref[...] = _rms(x_ref[...], g_ref[...]).astype(BF16)

    xv = xs_ref[...]
    ga = jnp.dot(xv, wa_ref[...], preferred_element_type=F32) + ba_ref[...]
    gate_ref[...] = jax.nn.gelu(ga).astype(gate_ref.dtype)
    rec_ref[...] = jnp.dot(xv, wb_ref[...], preferred_element_type=F32) + bb_ref[...]


def _inproj(x, gain, w_in, b_in, *, tm=512, tn=1024):
    nj = D_RNN // tn
    b2 = b_in.reshape(1, 2 * D_RNN)
    return pl.pallas_call(
        _inproj_kernel,
        grid=(TOKENS // tm, nj),
        in_specs=[pl.BlockSpec((tm, D_MODEL), lambda i, j: (i, 0)),
                  pl.BlockSpec((1, D_MODEL), lambda i, j: (0, 0)),
                  pl.BlockSpec((D_MODEL, tn), lambda i, j: (0, j)),
                  pl.BlockSpec((D_MODEL, tn), lambda i, j: (0, j + nj)),
                  pl.BlockSpec((1, tn), lambda i, j: (0, j)),
                  pl.BlockSpec((1, tn), lambda i, j: (0, j + nj))],
        out_specs=[pl.BlockSpec((tm, tn), lambda i, j: (i, j)),
                   pl.BlockSpec((tm, tn), lambda i, j: (i, j))],
        out_shape=[jax.ShapeDtypeStruct((TOKENS, D_RNN), BF16),
                   jax.ShapeDtypeStruct((TOKENS, D_RNN), F32)],
        scratch_shapes=[pltpu.VMEM((tm, D_MODEL), BF16)],
        compiler_params=_params("parallel", "arbitrary"),
        name="lru_in_proj",
    )(x, gain.reshape(1, D_MODEL), w_in, w_in, b2, b2)


def _rglru_kernel(x_ref, gate_ref, cw_ref, cb_ref, wg_ref, bg_ref, lam_ref, y_ref,
                  xp_s, a_s, b_s, tail_s, eq_s, cm_s, hc_s, *, chunk):
    c = pl.program_id(1)
    seg = chunk // SUBLANES
    halo = SUBLANES * (CONV_W - 1)

    @pl.when(c == 0)
    def _():
        tail_s[...] = jnp.zeros_like(tail_s)
        hc_s[...] = jnp.zeros_like(hc_s)

    for g in range(LANE_GROUPS):
        lanes = slice(g * LANES, (g + 1) * LANES)
        for j in range(SUBLANES):
            xp_s[g, pl.ds(halo + j, seg, stride=SUBLANES), :] = x_ref[j * seg:(j + 1) * seg, lanes]

    first_sublane = lax.broadcasted_iota(jnp.int32, (SUBLANES, LANES), 0) == 0
    for g in range(LANE_GROUPS):
        for d in range(1, CONV_W):
            slot = slice(SUBLANES * (CONV_W - 1 - d), SUBLANES * (CONV_W - d))
            cur = xp_s[g, halo + SUBLANES * (seg - d):halo + SUBLANES * (seg - d + 1), :]
            prev = tail_s[g, slot, :]
            xp_s[g, slot, :] = jnp.where(first_sublane, pltpu.roll(prev, 1, 0),
                                         pltpu.roll(cur, 1, 0))
            tail_s[g, slot, :] = cur

    lam = lam_ref[...]
    sp = jnp.maximum(-lam, 0.0) + jnp.log1p(jnp.exp(-jnp.abs(lam)))
    neg_half_c_sp = (-0.5 * LRU_C) * sp
    per_blk = LRU_BLOCK // LANES
    for n in range(N_BLK):
        parts = []
        for sub in range(per_blk):
            g = n * per_blk + sub
            lanes = slice(g * LANES, (g + 1) * LANES)
            acc = cb_ref[:, lanes] + cw_ref[CONV_W - 1:CONV_W, lanes] * xp_s[g, halo:halo + chunk, :]
            for d in range(1, CONV_W):
                off = halo - SUBLANES * d
                acc = acc + cw_ref[CONV_W - 1 - d:CONV_W - d, lanes] * xp_s[g, off:off + chunk, :]
            parts.append(acc)
        xn = jnp.concatenate(parts, axis=1)
        cols = slice(n * LRU_BLOCK, (n + 1) * LRU_BLOCK)
        gh = jnp.dot(xn.astype(BF16), wg_ref[n], preferred_element_type=F32) + bg_ref[n:n + 1, :]
        tr = jnp.tanh(gh[:, :LRU_BLOCK])
        i = 0.5 * jnp.tanh(gh[:, LRU_BLOCK:]) + 0.5
        log_a = (tr + 1.0) * neg_half_c_sp[:, cols]
        a = jnp.exp(log_a)
        x1 = -jnp.tanh(log_a) * (a * a + 1.0)
        mult = jnp.where(x1 > 0.0, x1 * lax.rsqrt(x1), 0.0)
        b = mult * (i * xn)
        for sub in range(per_blk):
            lanes = slice(sub * LANES, (sub + 1) * LANES)
            a_s[n * per_blk + sub] = a[:, lanes]
            b_s[n * per_blk + sub] = b[:, lanes]

    def tile(ref, g, s):
        return ref[g, pl.ds(pl.multiple_of(s * SUBLANES, SUBLANES), SUBLANES), :]

    for g0 in range(0, LANE_GROUPS, SCAN_GROUPS):
        groups = range(g0, g0 + SCAN_GROUPS)

        def pass1(s, carry):
            hs, ps = carry
            a = [tile(a_s, g, s) for g in groups]
            b = [tile(b_s, g, s) for g in groups]
            return (tuple(ai * hi + bi for ai, hi, bi in zip(a, hs, b)),
                    tuple(ai * pi for ai, pi in zip(a, ps)))

        zero = jnp.zeros((SUBLANES, LANES), F32)
        one = jnp.ones((SUBLANES, LANES), F32)
        e, q = lax.fori_loop(0, seg, pass1, ((zero,) * SCAN_GROUPS, (one,) * SCAN_GROUPS),
                             unroll=8)
        for idx, g in enumerate(groups):
            eq_s[0] = e[idx]
            eq_s[1] = q[idx]
            h = hc_s[g, 0:1, :]
            for j in range(SUBLANES):
                cm_s[g, j:j + 1, :] = h
                h = eq_s[1, j:j + 1, :] * h + eq_s[0, j:j + 1, :]
            hc_s[g, 0:1, :] = h

        def pass2(s, hs):
            out = []
            for g, h in zip(groups, hs):
                h = tile(a_s, g, s) * h + tile(b_s, g, s)
                b_s[g, pl.ds(pl.multiple_of(s * SUBLANES, SUBLANES), SUBLANES), :] = h
                out.append(h)
            return tuple(out)

        lax.fori_loop(0, seg, pass2, tuple(cm_s[g] for g in groups), unroll=8)

    for g in range(LANE_GROUPS):
        lanes = slice(g * LANES, (g + 1) * LANES)
        for j in range(SUBLANES):
            rows = slice(j * seg, (j + 1) * seg)
            h = b_s[g, pl.ds(j, seg, stride=SUBLANES), :]
            y_ref[rows, lanes] = gate_ref[rows, lanes] * h.astype(y_ref.dtype)


def _rglru(rec_pre, gate, conv_w, conv_b, w_gate, b_gate, lam, *, chunk=512):
    nc = SEQ // chunk
    row = lambda b, c: (b * nc + c, 0)
    full2 = lambda b, c: (0, 0)
    return pl.pallas_call(
        functools.partial(_rglru_kernel, chunk=chunk),
        grid=(BATCH, nc),
        in_specs=[pl.BlockSpec((chunk, D_RNN), row),
                  pl.BlockSpec((chunk, D_RNN), row),
                  pl.BlockSpec((CONV_W, D_RNN), full2),
                  pl.BlockSpec((1, D_RNN), full2),
                  pl.BlockSpec((N_BLK, LRU_BLOCK, 2 * LRU_BLOCK), lambda b, c: (0, 0, 0)),
                  pl.BlockSpec((N_BLK, 2 * LRU_BLOCK), full2),
                  pl.BlockSpec((1, D_RNN), full2)],
        out_specs=pl.BlockSpec((chunk, D_RNN), row),
        out_shape=jax.ShapeDtypeStruct((TOKENS, D_RNN), BF16),
        scratch_shapes=[pltpu.VMEM((LANE_GROUPS, SUBLANES * (CONV_W - 1) + chunk, LANES), F32),
                        pltpu.VMEM((LANE_GROUPS, chunk, LANES), F32),
                        pltpu.VMEM((LANE_GROUPS, chunk, LANES), F32),
                        pltpu.VMEM((LANE_GROUPS, SUBLANES * (CONV_W - 1), LANES), F32),
                        pltpu.VMEM((2, SUBLANES, LANES), F32),
                        pltpu.VMEM((LANE_GROUPS, SUBLANES, LANES), F32),
                        pltpu.VMEM((LANE_GROUPS, SUBLANES, LANES), F32)],
        compiler_params=_params("parallel", "arbitrary"),
        name="rglru_core",
    )(rec_pre, gate, conv_w, conv_b.reshape(1, D_RNN), w_gate, b_gate, lam.reshape(1, D_RNN))


def _ffn_kernel(h_ref, g_ref, wa_ref, wb_ref, w2_ref, o_ref, hn_s):
    f = pl.program_id(1)

    @pl.when(f == 0)
    def _():
        h = h_ref[...]
        hn_s[...] = _rms(h, g_ref[...]).astype(BF16)
        o_ref[...] = h

    hn = hn_s[...]
    ua = jnp.dot(hn, wa_ref[...], preferred_element_type=F32)
    ub = jnp.dot(hn, wb_ref[...], preferred_element_type=F32)
    act = (jax.nn.silu(ua) * ub).astype(BF16)
    o_ref[...] += jnp.dot(act, w2_ref[...], preferred_element_type=F32)


def _ffn(h, gain, w_in, w_out, layer, *, tm=1024, tf=512):
    nf = D_FF // tf
    return pl.pallas_call(
        _ffn_kernel,
        grid=(TOKENS // tm, nf),
        in_specs=[pl.BlockSpec((tm, D_MODEL), lambda i, f: (i, 0)),
                  pl.BlockSpec((1, D_MODEL), lambda i, f: (0, 0)),
                  pl.BlockSpec((None, D_MODEL, tf), lambda i, f: (layer, 0, f)),
                  pl.BlockSpec((None, D_MODEL, tf), lambda i, f: (layer, 0, f + nf)),
                  pl.BlockSpec((None, tf, D_MODEL), lambda i, f: (layer, f, 0))],
        out_specs=pl.BlockSpec((tm, D_MODEL), lambda i, f: (i, 0)),
        out_shape=jax.ShapeDtypeStruct((TOKENS, D_MODEL), F32),
        scratch_shapes=[pltpu.VMEM((tm, D_MODEL), BF16)],
        compiler_params=_params("parallel", "arbitrary"),
        name="swiglu_ffn",
    )(h, gain.reshape(1, D_MODEL), w_in, w_in, w_out)


def _kv_heads_kernel(c_ref, kr_ref, gl_ref, w_ref, gn_ref, gr_ref, rc_ref, rs_ref,
                     k_ref, v_ref, cn_s, krr_s, ssr_s, *, heads):
    @pl.when(pl.program_id(1) == 0)
    def _():
        cn_s[...] = _rms(c_ref[...], gl_ref[...]).astype(BF16)
        kr = kr_ref[...]
        ssr_s[...] = jnp.broadcast_to(jnp.sum(kr * kr, axis=1, keepdims=True), ssr_s.shape)
        krr_s[...] = _rope(kr * gr_ref[...], rc_ref[...], rs_ref[...])

    ones = jnp.ones((v_ref.shape[0], HEAD_PAD - V_D), v_ref.dtype)
    kv_all = jnp.dot(cn_s[...], w_ref[...], preferred_element_type=F32)
    for g in range(heads):
        base = g * HEAD_PAD
        kv = kv_all[:, base:base + HEAD_PAD]
        kn = kv[:, :NOPE_D]
        ss = jnp.sum(kn * kn, axis=1, keepdims=True) + ssr_s[...]
        rstd = lax.rsqrt(ss * (1.0 / QK_D) + NORM_EPS)
        k_ref[:, base:base + NOPE_D] = (kn * rstd * gn_ref[...]).astype(k_ref.dtype)
        k_ref[:, base + NOPE_D:base + HEAD_PAD] = (krr_s[...] * rstd).astype(k_ref.dtype)
        v_ref[:, base:base + V_D] = kv[:, NOPE_D:].astype(v_ref.dtype)
        v_ref[:, base + V_D:base + HEAD_PAD] = ones


def _kv_heads(ckr, gl, w_ukv, gn, gr, rc, rs, *, tm=1024, heads=8):
    rope_blk = KV_LORA // LANES
    row = lambda i, h: (i, 0)
    vec = lambda i, h: (0, 0)
    return pl.pallas_call(
        functools.partial(_kv_heads_kernel, heads=heads),
        grid=(TOKENS // tm, N_HEADS // heads),
        in_specs=[pl.BlockSpec((tm, KV_LORA), row),
                  pl.BlockSpec((tm, LANES), lambda i, h: (i, rope_blk)),
                  pl.BlockSpec((1, KV_LORA), vec),
                  pl.BlockSpec((KV_LORA, heads * HEAD_PAD), lambda i, h: (0, h)),
                  pl.BlockSpec((1, LANES), vec),
                  pl.BlockSpec((1, LANES), vec),
                  pl.BlockSpec((tm, LANES), row),
                  pl.BlockSpec((tm, LANES), row)],
        out_specs=[pl.BlockSpec((tm, heads * HEAD_PAD), lambda i, h: (i, h)),
                   pl.BlockSpec((tm, heads * HEAD_PAD), lambda i, h: (i, h))],
        out_shape=[jax.ShapeDtypeStruct((TOKENS, N_HEADS * HEAD_PAD), BF16),
                   jax.ShapeDtypeStruct((TOKENS, N_HEADS * HEAD_PAD), BF16)],
        scratch_shapes=[pltpu.VMEM((tm, KV_LORA), BF16),
                        pltpu.VMEM((tm, LANES), F32),
                        pltpu.VMEM((tm, LANES), F32)],
        compiler_params=_params("parallel", "arbitrary"),
        name="kv_heads",
    )(ckr, ckr, gl.reshape(1, KV_LORA), w_ukv, gn, gr, rc, rs)


def _q_heads_kernel(c_ref, gl_ref, w_ref, gn_ref, gr_ref, gs_ref, rc_ref, rs_ref,
                    q_ref, cn_s, cg_s, sg_s, *, heads):
    @pl.when(pl.program_id(1) == 0)
    def _():
        cn_s[...] = _rms(c_ref[...], gl_ref[...]).astype(BF16)
        cg_s[...] = rc_ref[...] * gr_ref[...]
        sg_s[...] = rs_ref[...] * gs_ref[...]

    q_all = jnp.dot(cn_s[...], w_ref[...], preferred_element_type=F32)
    for g in range(heads):
        q = q_all[:, g * Q_HEAD_COLS:(g + 1) * Q_HEAD_COLS]
        qn = q[:, :NOPE_D]
        qr = q[:, NOPE_D:NOPE_D + LANES]
        qs = q[:, NOPE_D + LANES:]
        ss = jnp.sum(qn * qn + qr * qr, axis=1, keepdims=True)
        rstd = lax.rsqrt(ss * (1.0 / QK_D) + NORM_EPS)
        base = g * HEAD_PAD
        q_ref[:, base:base + NOPE_D] = (qn * rstd * gn_ref[...]).astype(q_ref.dtype)
        roped = qr * cg_s[...] + qs * sg_s[...]
        q_ref[:, base + NOPE_D:base + HEAD_PAD] = (roped * rstd).astype(q_ref.dtype)


def _q_heads(cq, gl, w_uq3, gn, gr, gs, rc, rs, *, tm=1024, heads=8):
    row = lambda i, h: (i, 0)
    vec = lambda i, h: (0, 0)
    return pl.pallas_call(
        functools.partial(_q_heads_kernel, heads=heads),
        grid=(TOKENS // tm, N_HEADS // heads),
        in_specs=[pl.BlockSpec((tm, Q_LORA), row),
                  pl.BlockSpec((1, Q_LORA), vec),
                  pl.BlockSpec((Q_LORA, heads * Q_HEAD_COLS), lambda i, h: (0, h)),
                  pl.BlockSpec((1, LANES), vec),
                  pl.BlockSpec((1, LANES), vec),
                  pl.BlockSpec((1, LANES), vec),
                  pl.BlockSpec((tm, LANES), row),
                  pl.BlockSpec((tm, LANES), row)],
        out_specs=pl.BlockSpec((tm, heads * HEAD_PAD), lambda i, h: (i, h)),
        out_shape=jax.ShapeDtypeStruct((TOKENS, N_HEADS * HEAD_PAD), BF16),
        scratch_shapes=[pltpu.VMEM((tm, Q_LORA), BF16),
                        pltpu.VMEM((tm, LANES), F32),
                        pltpu.VMEM((tm, LANES), F32)],
        compiler_params=_params("parallel", "arbitrary"),
        name="q_heads",
    )(cq, gl.reshape(1, Q_LORA), w_uq3, gn, gr, gs, rc, rs)


def _attn_kernel(q_ref, k_ref, v_ref, o_ref, m_s, acc_s, *, blk, heads, subs):
    i = pl.program_id(2)
    m_s[...] = jnp.full_like(m_s, NEG_BIG)
    acc_s[...] = jnp.zeros_like(acc_s)

    def chain(g, u, start, masked):
        cols = slice(g * HEAD_PAD, (g + 1) * HEAD_PAD)
        rows = slice(u * blk, (u + 1) * blk)
        k = k_ref[pl.ds(start, blk), cols]
        v = v_ref[pl.ds(start, blk), cols]
        s = lax.dot_general(q_ref[rows, cols], k, (((1,), (1,)), ((), ())),
                            preferred_element_type=F32)
        if masked:
            row = lax.broadcasted_iota(jnp.int32, (blk, blk), 0)
            col = lax.broadcasted_iota(jnp.int32, (blk, blk), 1)
            s = jnp.where(col <= row, s, NEG_BIG)
        m_prev = m_s[g, rows, :]
        m_new = jnp.maximum(m_prev, jnp.max(s, axis=1, keepdims=True))
        p = jnp.exp2(s - jnp.concatenate([m_new] * (blk // LANES), axis=1))
        alpha = jnp.exp2(m_prev - m_new)
        pv = jnp.dot(p.astype(BF16), v, preferred_element_type=F32)
        acc_s[g, rows, :] = (jnp.concatenate([alpha] * (HEAD_PAD // LANES), axis=1)
                             * acc_s[g, rows, :] + pv)
        m_s[g, rows, :] = m_new

    def step(j, active):
        start = pl.multiple_of(j * blk, blk)
        for g in range(heads):
            for u, masked in active:
                chain(g, u, start, masked)

    def body(j, carry):
        step(j, [(u, False) for u in range(subs)])
        return carry

    lax.fori_loop(0, subs * i, body, 0)
    for d in range(subs):
        step(subs * i + d, [(d, True)] + [(u, False) for u in range(d + 1, subs)])
    for g in range(heads):
        acc = acc_s[g]
        o_ref[:, g * V_D:(g + 1) * V_D] = (acc[:, :V_D] / acc[:, V_D:]).astype(o_ref.dtype)


def _attention(q, k, v, *, blk=512, heads=2, subs=4):
    tq = subs * blk
    nq = SEQ // tq
    return pl.pallas_call(
        functools.partial(_attn_kernel, blk=blk, heads=heads, subs=subs),
        grid=(BATCH, N_HEADS // heads, nq),
        in_specs=[pl.BlockSpec((tq, heads * HEAD_PAD), lambda b, h, i: (b * nq + i, h)),
                  pl.BlockSpec((SEQ, heads * HEAD_PAD), lambda b, h, i: (b, h)),
                  pl.BlockSpec((SEQ, heads * HEAD_PAD), lambda b, h, i: (b, h))],
        out_specs=pl.BlockSpec((tq, heads * V_D), lambda b, h, i: (b * nq + i, h)),
        out_shape=jax.ShapeDtypeStruct((TOKENS, N_HEADS * V_D), BF16),
        scratch_shapes=[pltpu.VMEM((heads, tq, LANES), F32),
                        pltpu.VMEM((heads, tq, HEAD_PAD), F32)],
        compiler_params=_params("parallel", "parallel", "arbitrary"),
        name="flash_attention",
    )(q, k, v)


def _swap_halves(t):
    half = ROPE_D // 2
    return jnp.concatenate([t[..., half:], t[..., :half]], axis=-1)


def _split_qk_gain(g, scale):
    pad = lambda v: jnp.pad(v * scale, (0, LANES - ROPE_D)).reshape(1, LANES)
    return (g[:NOPE_D] * scale).reshape(1, LANES), pad(g[NOPE_D:]), pad(_swap_halves(g[NOPE_D:]))


def kernel(x, positions, norm_mix, norm_ffn, lru_w_in, lru_b_in, lru_conv_w, lru_conv_b, lru_w_gate, lru_b_gate, lru_lambda, lru_w_out, lru_b_out, kv_norm_in, w_dkv, kv_latent_norm, w_ukv, k_norm, w_dq, q_latent_norm, w_uq, q_norm, w_o, ffn_w_in, ffn_w_out):
    h = x.reshape(TOKENS, D_MODEL)
    rc, rs = _rope_tables(positions)

    gate, rec_pre = _inproj(h, norm_mix[0], lru_w_in[0].astype(BF16), lru_b_in[0])
    y = _rglru(rec_pre, gate, lru_conv_w[0], lru_conv_b[0], (0.5 * lru_w_gate[0]).astype(BF16),
               0.5 * lru_b_gate[0], lru_lambda[0])
    h = _linear(y, lru_w_out[0].astype(BF16), bias=lru_b_out[0], resid=h, tn=D_MODEL,
                name="lru_out_proj")
    ffn_w_in_b = ffn_w_in.astype(BF16)
    ffn_w_out_b = ffn_w_out.astype(BF16)
    h = _ffn(h, norm_ffn[0], ffn_w_in_b, ffn_w_out_b, 0)

    w_dkv_p = jnp.pad(w_dkv, ((0, 0), (0, LANES - ROPE_D))).astype(BF16)
    ckr, cq = _down_proj(h, kv_norm_in, norm_mix[1], w_dkv_p, w_dq[0].astype(BF16))
    kgn, kgr, _ = _split_qk_gain(k_norm, 1.0)
    k_sh, v_sh = _kv_heads(ckr, kv_latent_norm,
                           w_ukv.reshape(KV_LORA, N_HEADS * (NOPE_D + V_D)).astype(BF16),
                           kgn, kgr, rc, rs)

    zpad = jnp.zeros((Q_LORA, N_HEADS, LANES - ROPE_D), F32)
    w_rope = w_uq[0][..., NOPE_D:]
    w_uq3 = jnp.concatenate([w_uq[0][..., :NOPE_D], w_rope, zpad, _swap_halves(w_rope), zpad],
                            axis=-1)
    qgn, qgr, qgs = _split_qk_gain(q_norm[0], math.log2(math.e) / math.sqrt(QK_D))
    q = _q_heads(cq, q_latent_norm[0],
                 w_uq3.reshape(Q_LORA, N_HEADS * Q_HEAD_COLS).astype(BF16),
                 qgn, qgr, qgs, rc, rs)
    o = _attention(q, k_sh, v_sh)
    h = _linear(o, w_o[0].astype(BF16), resid=h, tn=D_MODEL, name="attn_out_proj")
    h = _ffn(h, norm_ffn[1], ffn_w_in_b, ffn_w_out_b, 1)
    return h.reshape(BATCH, SEQ, D_MODEL)
```

```python
import functools
import math

import jax
import jax.numpy as jnp
from jax import lax
from jax.experimental import pallas as pl
from jax.experimental.pallas import tpu as pltpu

F32 = jnp.float32
BF16 = jnp.bfloat16

D_MODEL = 2048
BATCH = 4
SEQ = 4096
TOKENS = BATCH * SEQ
D_RNN = D_MODEL
LRU_BLOCK = 256
N_BLK = D_RNN // LRU_BLOCK
CONV_W = 4
LRU_C = 8.0
N_HEADS = 16
NOPE_D = 128
ROPE_D = 64
QK_D = NOPE_D + ROPE_D
V_D = 128
Q_LORA = 512
KV_LORA = 512
ROPE_THETA = 10000.0
D_FF = 5632
NORM_EPS = 1e-6

LANES = 128
SUBLANES = 8
HEAD_PAD = 2 * LANES
DOWN_PROJ_CHUNKS = 2
Q_HEAD_COLS = 3 * LANES
LANE_GROUPS = D_RNN // LANES
SCAN_GROUPS = 8
V7X_VMEM_BYTES = 64 * 1024 * 1024
VMEM_LIMIT = V7X_VMEM_BYTES * 7 // 8
NEG_BIG = -1e30


def _params(*semantics):
    return pltpu.CompilerParams(dimension_semantics=semantics, vmem_limit_bytes=VMEM_LIMIT)


def _rms(x, g):
    ms = jnp.mean(x * x, axis=-1, keepdims=True)
    return x * lax.rsqrt(ms + NORM_EPS) * g


def _rope(t, c, s):
    swapped = pltpu.roll(t, LANES - ROPE_D // 2, 1) + pltpu.roll(t, ROPE_D // 2, 1)
    return t * c + swapped * s


def _rope_table_kernel(pos_ref, invf_ref, cos_ref, sin_ref):
    ang = pos_ref[...].astype(F32) * invf_ref[...]
    cos_ref[...] = jnp.cos(ang)
    sin_ref[...] = jnp.sin(ang)


def _rope_tables(positions):
    half = ROPE_D // 2
    per_row = LANES // half
    rows = TOKENS // per_row
    inv_freq = ROPE_THETA ** (-jnp.arange(0, ROPE_D, 2, dtype=F32) / ROPE_D)
    pos = jnp.broadcast_to(positions.reshape(rows, per_row, 1), (rows, per_row, half))
    pos = pos.reshape(rows, LANES)
    invf = jnp.tile(inv_freq, per_row).reshape(1, LANES)
    tr = 512
    cosp, sinp = pl.pallas_call(
        _rope_table_kernel,
        grid=(rows // tr,),
        in_specs=[pl.BlockSpec((tr, LANES), lambda i: (i, 0)),
                  pl.BlockSpec((1, LANES), lambda i: (0, 0))],
        out_specs=[pl.BlockSpec((tr, LANES), lambda i: (i, 0)),
                   pl.BlockSpec((tr, LANES), lambda i: (i, 0))],
        out_shape=[jax.ShapeDtypeStruct((rows, LANES), F32)] * 2,
        compiler_params=_params("arbitrary"),
        name="rope_tables",
    )(pos, invf)
    cos = cosp.reshape(TOKENS, half)
    sin = sinp.reshape(TOKENS, half)
    z = jnp.zeros((TOKENS, LANES - ROPE_D), F32)
    rc = jnp.concatenate([cos, cos, z], axis=1)
    rs = jnp.concatenate([-sin, sin, z], axis=1)
    return rc, rs


def _linear_kernel(*refs, has_gain, has_bias, has_resid, stage_x):
    refs = list(refs)
    x_ref = refs.pop(0)
    w_ref = refs.pop(0)
    g_ref = refs.pop(0) if has_gain else None
    b_ref = refs.pop(0) if has_bias else None
    r_ref = refs.pop(0) if has_resid else None
    o_ref = refs.pop(0)
    if stage_x:
        xs_ref = refs.pop(0)

        @pl.when(pl.program_id(1) == 0)
        def _():
            x = x_ref[...].astype(F32)
            if has_gain:
                x = _rms(x, g_ref[...])
            xs_ref[...] = x.astype(BF16)

        xv = xs_ref[...]
    else:
        xv = x_ref[...]
    acc = jnp.dot(xv, w_ref[...], preferred_element_type=F32)
    if has_bias:
        acc = acc + b_ref[...]
    if has_resid:
        acc = acc + r_ref[...]
    o_ref[...] = acc.astype(o_ref.dtype)


def _linear(x, w, *, gain=None, bias=None, resid=None, out_dtype=F32, tm=512, tn=1024, name):
    m, k = x.shape
    n = w.shape[1]
    tn = min(tn, n)
    stage_x = gain is not None or x.dtype != BF16
    in_specs = [pl.BlockSpec((tm, k), lambda i, j: (i, 0)),
                pl.BlockSpec((k, tn), lambda i, j: (0, j))]
    args = [x, w]
    if gain is not None:
        in_specs.append(pl.BlockSpec((1, k), lambda i, j: (0, 0)))
        args.append(gain.reshape(1, k))
    if bias is not None:
        in_specs.append(pl.BlockSpec((1, tn), lambda i, j: (0, j)))
        args.append(bias.reshape(1, n))
    if resid is not None:
        in_specs.append(pl.BlockSpec((tm, tn), lambda i, j: (i, j)))
        args.append(resid)
    scratch = [pltpu.VMEM((tm, k), BF16)] if stage_x else []
    return pl.pallas_call(
        functools.partial(_linear_kernel, has_gain=gain is not None, has_bias=bias is not None,
                          has_resid=resid is not None, stage_x=stage_x),
        grid=(m // tm, n // tn),
        in_specs=in_specs,
        out_specs=pl.BlockSpec((tm, tn), lambda i, j: (i, j)),
        out_shape=jax.ShapeDtypeStruct((m, n), out_dtype),
        scratch_shapes=scratch,
        compiler_params=_params("parallel", "arbitrary"),
        name=name,
    )(*args)


def _down_proj_kernel(x_ref, gkv_ref, gq_ref, wkv_ref, wq_ref, ckr_ref, cq_ref):
    chunk = x_ref.shape[0] // DOWN_PROJ_CHUNKS
    for r in range(DOWN_PROJ_CHUNKS):
        rows = slice(r * chunk, (r + 1) * chunk)
        x = x_ref[rows, :]
        xhat = x * lax.rsqrt(jnp.mean(x * x, axis=-1, keepdims=True) + NORM_EPS)
        ckr_ref[rows, :] = jnp.dot((xhat * gkv_ref[...]).astype(BF16), wkv_ref[...],
                                   preferred_element_type=F32)
        cq_ref[rows, :] = jnp.dot((xhat * gq_ref[...]).astype(BF16), wq_ref[...],
                                  preferred_element_type=F32)


def _down_proj(x, g_kv, g_q, w_dkv, w_dq, *, tm=1024):
    n_kv = w_dkv.shape[1]
    row = lambda i: (i, 0)
    full = lambda i: (0, 0)
    return pl.pallas_call(
        _down_proj_kernel,
        grid=(TOKENS // tm,),
        in_specs=[pl.BlockSpec((tm, D_MODEL), row),
                  pl.BlockSpec((1, D_MODEL), full),
                  pl.BlockSpec((1, D_MODEL), full),
                  pl.BlockSpec((D_MODEL, n_kv), full),
                  pl.BlockSpec((D_MODEL, Q_LORA), full)],
        out_specs=[pl.BlockSpec((tm, n_kv), row),
                   pl.BlockSpec((tm, Q_LORA), row)],
        out_shape=[jax.ShapeDtypeStruct((TOKENS, n_kv), F32),
                   jax.ShapeDtypeStruct((TOKENS, Q_LORA), F32)],
        compiler_params=_params("parallel"),
        name="mla_down_proj",
    )(x, g_kv.reshape(1, D_MODEL), g_q.reshape(1, D_MODEL), w_dkv, w_dq)


def _inproj_kernel(x_ref, g_ref, wa_ref, wb_ref, ba_ref, bb_ref, w1_ref, w2_ref,
                   gate_ref, rec_ref, w1_b_ref, w2_b_ref, xs_ref):
    w1_b_ref[...] = w1_ref[...].astype(w1_b_ref.dtype)
    w2_b_ref[...] = w2_ref[...].astype(w2_b_ref.dtype)

    @pl.when(pl.program_id(1) == 0)
    def _():
        xs_ref[...] = _rms(x_ref[...], g_ref[...]).astype(BF16)

    xv = xs_ref[...]
    ga = jnp.dot(xv, wa_ref[...], preferred_element_type=F32) + ba_ref[...]
    gate_ref[...] = jax.nn.gelu(ga).astype(gate_ref.dtype)
    rec_ref[...] = jnp.dot(xv, wb_ref[...], preferred_element_type=F32) + bb_ref[...]


def _inproj(x, gain, w_in, b_in, sq1, sq2, *, tm=512, tn=1024):
    ni = TOKENS // tm
    nj = D_RNN // tn
    b2 = b_in.reshape(1, 2 * D_RNN)
    sq_blk = (D_MODEL // ni, D_MODEL // nj)
    sq_in = pl.BlockSpec((None,) + sq_blk, lambda i, j: (0, i, j))
    sq_out = pl.BlockSpec(sq_blk, lambda i, j: (i, j))
    sq_shape = jax.ShapeDtypeStruct((D_MODEL, D_MODEL), BF16)
    return pl.pallas_call(
        _inproj_kernel,
        grid=(ni, nj),
        in_specs=[pl.BlockSpec((tm, D_MODEL), lambda i, j: (i, 0)),
                  pl.BlockSpec((1, D_MODEL), lambda i, j: (0, 0)),
                  pl.BlockSpec((D_MODEL, tn), lambda i, j: (0, j)),
                  pl.BlockSpec((D_MODEL, tn), lambda i, j: (0, j + nj)),
                  pl.BlockSpec((1, tn), lambda i, j: (0, j)),
                  pl.BlockSpec((1, tn), lambda i, j: (0, j + nj)),
                  sq_in, sq_in],
        out_specs=[pl.BlockSpec((tm, tn), lambda i, j: (i, j)),
                   pl.BlockSpec((tm, tn), lambda i, j: (i, j)),
                   sq_out, sq_out],
        out_shape=[jax.ShapeDtypeStruct((TOKENS, D_RNN), BF16),
                   jax.ShapeDtypeStruct((TOKENS, D_RNN), F32),
                   sq_shape, sq_shape],
        scratch_shapes=[pltpu.VMEM((tm, D_MODEL), BF16)],
        compiler_params=_params("parallel", "arbitrary"),
        name="lru_in_proj",
    )(x, gain.reshape(1, D_MODEL), w_in, w_in, b2, b2, sq1, sq2)


def _rglru_kernel(x_ref, gate_ref, cw_ref, cb_ref, wg_ref, bg_ref, lam_ref, wi_ref, wo_ref,
                  y_ref, wi_b_ref, wo_b_ref,
                  xp_s, a_s, b_s, tail_s, eq_s, cm_s, hc_s, *, chunk):
    wi_b_ref[...] = wi_ref[...].astype(wi_b_ref.dtype)
    wo_b_ref[...] = wo_ref[...].astype(wo_b_ref.dtype)

    c = pl.program_id(1)
    seg = chunk // SUBLANES
    halo = SUBLANES * (CONV_W - 1)

    @pl.when(c == 0)
    def _():
        tail_s[...] = jnp.zeros_like(tail_s)
        hc_s[...] = jnp.zeros_like(hc_s)

    for g in range(LANE_GROUPS):
        lanes = slice(g * LANES, (g + 1) * LANES)
        for j in range(SUBLANES):
            xp_s[g, pl.ds(halo + j, seg, stride=SUBLANES), :] = x_ref[j * seg:(j + 1) * seg, lanes]

    first_sublane = lax.broadcasted_iota(jnp.int32, (SUBLANES, LANES), 0) == 0
    for g in range(LANE_GROUPS):
        for d in range(1, CONV_W):
            slot = slice(SUBLANES * (CONV_W - 1 - d), SUBLANES * (CONV_W - d))
            cur = xp_s[g, halo + SUBLANES * (seg - d):halo + SUBLANES * (seg - d + 1), :]
            prev = tail_s[g, slot, :]
            xp_s[g, slot, :] = jnp.where(first_sublane, pltpu.roll(prev, 1, 0),
                                         pltpu.roll(cur, 1, 0))
            tail_s[g, slot, :] = cur

    lam = lam_ref[...]
    sp = jnp.maximum(-lam, 0.0) + jnp.log1p(jnp.exp(-jnp.abs(lam)))
    neg_half_c_sp = (-0.5 * LRU_C) * sp
    per_blk = LRU_BLOCK // LANES
    for n in range(N_BLK):
        parts = []
        for sub in range(per_blk):
            g = n * per_blk + sub
            lanes = slice(g * LANES, (g + 1) * LANES)
            acc = cb_ref[:, lanes] + cw_ref[CONV_W - 1:CONV_W, lanes] * xp_s[g, halo:halo + chunk, :]
            for d in range(1, CONV_W):
                off = halo - SUBLANES * d
                acc = acc + cw_ref[CONV_W - 1 - d:CONV_W - d, lanes] * xp_s[g, off:off + chunk, :]
            parts.append(acc)
        xn = jnp.concatenate(parts, axis=1)
        cols = slice(n * LRU_BLOCK, (n + 1) * LRU_BLOCK)
        gh = jnp.dot(xn.astype(BF16), wg_ref[n], preferred_element_type=F32) + bg_ref[n:n + 1, :]
        tr = jnp.tanh(gh[:, :LRU_BLOCK])
        i = 0.5 * jnp.tanh(gh[:, LRU_BLOCK:]) + 0.5
        log_a = (tr + 1.0) * neg_half_c_sp[:, cols]
        a = jnp.exp(log_a)
        x1 = -jnp.tanh(log_a) * (a * a + 1.0)
        mult = jnp.where(x1 > 0.0, x1 * lax.rsqrt(x1), 0.0)
        b = mult * (i * xn)
        for sub in range(per_blk):
            lanes = slice(sub * LANES, (sub + 1) * LANES)
            a_s[n * per_blk + sub] = a[:, lanes]
            b_s[n * per_blk + sub] = b[:, lanes]

    def tile(ref, g, s):
        return ref[g, pl.ds(pl.multiple_of(s * SUBLANES, SUBLANES), SUBLANES), :]

    for g0 in range(0, LANE_GROUPS, SCAN_GROUPS):
        groups = range(g0, g0 + SCAN_GROUPS)

        def pass1(s, carry):
            hs, ps = carry
            a = [tile(a_s, g, s) for g in groups]
            b = [tile(b_s, g, s) for g in groups]
            return (tuple(ai * hi + bi for ai, hi, bi in zip(a, hs, b)),
                    tuple(ai * pi for ai, pi in zip(a, ps)))

        zero = jnp.zeros((SUBLANES, LANES), F32)
        one = jnp.ones((SUBLANES, LANES), F32)
        e, q = lax.fori_loop(0, seg, pass1, ((zero,) * SCAN_GROUPS, (one,) * SCAN_GROUPS),
                             unroll=8)
        for idx, g in enumerate(groups):
            eq_s[0] = e[idx]
            eq_s[1] = q[idx]
            h = hc_s[g, 0:1, :]
            for j in range(SUBLANES):
                cm_s[g, j:j + 1, :] = h
                h = eq_s[1, j:j + 1, :] * h + eq_s[0, j:j + 1, :]
            hc_s[g, 0:1, :] = h

        def pass2(s, hs):
            out = []
            for g, h in zip(groups, hs):
                h = tile(a_s, g, s) * h + tile(b_s, g, s)
                b_s[g, pl.ds(pl.multiple_of(s * SUBLANES, SUBLANES), SUBLANES), :] = h
                out.append(h)
            return tuple(out)

        lax.fori_loop(0, seg, pass2, tuple(cm_s[g] for g in groups), unroll=8)

    for g in range(LANE_GROUPS):
        lanes = slice(g * LANES, (g + 1) * LANES)
        for j in range(SUBLANES):
            rows = slice(j * seg, (j + 1) * seg)
            h = b_s[g, pl.ds(j, seg, stride=SUBLANES), :]
            y_ref[rows, lanes] = gate_ref[rows, lanes] * h.astype(y_ref.dtype)


def _rglru(rec_pre, gate, conv_w, conv_b, w_gate, b_gate, lam, ffn_w_in, ffn_w_out, layer, *,
           chunk=512):
    nc = SEQ // chunk
    steps = BATCH * nc
    wi_rows = D_MODEL // steps
    wo_rows = D_FF // steps
    row = lambda b, c: (b * nc + c, 0)
    band = lambda b, c: (layer, b * nc + c, 0)
    full2 = lambda b, c: (0, 0)
    return pl.pallas_call(
        functools.partial(_rglru_kernel, chunk=chunk),
        grid=(BATCH, nc),
        in_specs=[pl.BlockSpec((chunk, D_RNN), row),
                  pl.BlockSpec((chunk, D_RNN), row),
                  pl.BlockSpec((CONV_W, D_RNN), full2),
                  pl.BlockSpec((1, D_RNN), full2),
                  pl.BlockSpec((N_BLK, LRU_BLOCK, 2 * LRU_BLOCK), lambda b, c: (0, 0, 0)),
                  pl.BlockSpec((N_BLK, 2 * LRU_BLOCK), full2),
                  pl.BlockSpec((1, D_RNN), full2),
                  pl.BlockSpec((None, wi_rows, 2 * D_FF), band),
                  pl.BlockSpec((None, wo_rows, D_MODEL), band)],
        out_specs=[pl.BlockSpec((chunk, D_RNN), row),
                   pl.BlockSpec((wi_rows, 2 * D_FF), row),
                   pl.BlockSpec((wo_rows, D_MODEL), row)],
        out_shape=[jax.ShapeDtypeStruct((TOKENS, D_RNN), BF16),
                   jax.ShapeDtypeStruct((D_MODEL, 2 * D_FF), BF16),
                   jax.ShapeDtypeStruct((D_FF, D_MODEL), BF16)],
        scratch_shapes=[pltpu.VMEM((LANE_GROUPS, SUBLANES * (CONV_W - 1) + chunk, LANES), F32),
                        pltpu.VMEM((LANE_GROUPS, chunk, LANES), F32),
                        pltpu.VMEM((LANE_GROUPS, chunk, LANES), F32),
                        pltpu.VMEM((LANE_GROUPS, SUBLANES * (CONV_W - 1), LANES), F32),
                        pltpu.VMEM((2, SUBLANES, LANES), F32),
                        pltpu.VMEM((LANE_GROUPS, SUBLANES, LANES), F32),
                        pltpu.VMEM((LANE_GROUPS, SUBLANES, LANES), F32)],
        compiler_params=_params("parallel", "arbitrary"),
        name="rglru_core",
    )(rec_pre, gate, conv_w, conv_b.reshape(1, D_RNN), w_gate, b_gate, lam.reshape(1, D_RNN),
      ffn_w_in, ffn_w_out)


def _ffn_kernel(h_ref, g_ref, wa_ref, wb_ref, w2_ref, o_ref, hn_s):
    f = pl.program_id(1)

    @pl.when(f == 0)
    def _():
        h = h_ref[...]
        hn_s[...] = _rms(h, g_ref[...]).astype(BF16)
        o_ref[...] = h

    hn = hn_s[...]
    ua = jnp.dot(hn, wa_ref[...], preferred_element_type=F32)
    ub = jnp.dot(hn, wb_ref[...], preferred_element_type=F32)
    act = (jax.nn.silu(ua) * ub).astype(BF16)
    o_ref[...] += jnp.dot(act, w2_ref[...], preferred_element_type=F32)


def _ffn(h, gain, w_in, w_out, *, tm=1024, tf=512):
    nf = D_FF // tf
    return pl.pallas_call(
        _ffn_kernel,
        grid=(TOKENS // tm, nf),
        in_specs=[pl.BlockSpec((tm, D_MODEL), lambda i, f: (i, 0)),
                  pl.BlockSpec((1, D_MODEL), lambda i, f: (0, 0)),
                  pl.BlockSpec((D_MODEL, tf), lambda i, f: (0, f)),
                  pl.BlockSpec((D_MODEL, tf), lambda i, f: (0, f + nf)),
                  pl.BlockSpec((tf, D_MODEL), lambda i, f: (f, 0))],
        out_specs=pl.BlockSpec((tm, D_MODEL), lambda i, f: (i, 0)),
        out_shape=jax.ShapeDtypeStruct((TOKENS, D_MODEL), F32),
        scratch_shapes=[pltpu.VMEM((tm, D_MODEL), BF16)],
        compiler_params=_params("parallel", "arbitrary"),
        name="swiglu_ffn",
    )(h, gain.reshape(1, D_MODEL), w_in, w_in, w_out)


def _kv_heads_kernel(c_ref, kr_ref, gl_ref, w_ref, gn_ref, gr_ref, rc_ref, rs_ref,
                     k_ref, v_ref, cn_s, krr_s, ssr_s, *, heads):
    @pl.when(pl.program_id(1) == 0)
    def _():
        cn_s[...] = _rms(c_ref[...], gl_ref[...]).astype(BF16)
        kr = kr_ref[...]
        ssr_s[...] = jnp.broadcast_to(jnp.sum(kr * kr, axis=1, keepdims=True), ssr_s.shape)
        krr_s[...] = _rope(kr * gr_ref[...], rc_ref[...], rs_ref[...])

    ones = jnp.ones((v_ref.shape[0], HEAD_PAD - V_D), v_ref.dtype)
    kv_all = jnp.dot(cn_s[...], w_ref[...], preferred_element_type=F32)
    for g in range(heads):
        base = g * HEAD_PAD
        kv = kv_all[:, base:base + HEAD_PAD]
        kn = kv[:, :NOPE_D]
        ss = jnp.sum(kn * kn, axis=1, keepdims=True) + ssr_s[...]
        rstd = lax.rsqrt(ss * (1.0 / QK_D) + NORM_EPS)
        k_ref[:, base:base + NOPE_D] = (kn * rstd * gn_ref[...]).astype(k_ref.dtype)
        k_ref[:, base + NOPE_D:base + HEAD_PAD] = (krr_s[...] * rstd).astype(k_ref.dtype)
        v_ref[:, base:base + V_D] = kv[:, NOPE_D:].astype(v_ref.dtype)
        v_ref[:, base + V_D:base + HEAD_PAD] = ones


def _kv_heads(ckr, gl, w_ukv, gn, gr, rc, rs, *, tm=1024, heads=8):
    rope_blk = KV_LORA // LANES
    row = lambda i, h: (i, 0)
    vec = lambda i, h: (0, 0)
    return pl.pallas_call(
        functools.partial(_kv_heads_kernel, heads=heads),
        grid=(TOKENS // tm, N_HEADS // heads),
        in_specs=[pl.BlockSpec((tm, KV_LORA), row),
                  pl.BlockSpec((tm, LANES), lambda i, h: (i, rope_blk)),
                  pl.BlockSpec((1, KV_LORA), vec),
                  pl.BlockSpec((KV_LORA, heads * HEAD_PAD), lambda i, h: (0, h)),
                  pl.BlockSpec((1, LANES), vec),
                  pl.BlockSpec((1, LANES), vec),
                  pl.BlockSpec((tm, LANES), row),
                  pl.BlockSpec((tm, LANES), row)],
        out_specs=[pl.BlockSpec((tm, heads * HEAD_PAD), lambda i, h: (i, h)),
                   pl.BlockSpec((tm, heads * HEAD_PAD), lambda i, h: (i, h))],
        out_shape=[jax.ShapeDtypeStruct((TOKENS, N_HEADS * HEAD_PAD), BF16),
                   jax.ShapeDtypeStruct((TOKENS, N_HEADS * HEAD_PAD), BF16)],
        scratch_shapes=[pltpu.VMEM((tm, KV_LORA), BF16),
                        pltpu.VMEM((tm, LANES), F32),
                        pltpu.VMEM((tm, LANES), F32)],
        compiler_params=_params("parallel", "arbitrary"),
        name="kv_heads",
    )(ckr, ckr, gl.reshape(1, KV_LORA), w_ukv, gn, gr, rc, rs)


def _q_heads_kernel(c_ref, gl_ref, w_ref, gn_ref, gr_ref, gs_ref, rc_ref, rs_ref,
                    q_ref, cn_s, cg_s, sg_s, *, heads):
    @pl.when(pl.program_id(1) == 0)
    def _():
        cn_s[...] = _rms(c_ref[...], gl_ref[...]).astype(BF16)
        cg_s[...] = rc_ref[...] * gr_ref[...]
        sg_s[...] = rs_ref[...] * gs_ref[...]

    q_all = jnp.dot(cn_s[...], w_ref[...], preferred_element_type=F32)
    for g in range(heads):
        q = q_all[:, g * Q_HEAD_COLS:(g + 1) * Q_HEAD_COLS]
        qn = q[:, :NOPE_D]
        qr = q[:, NOPE_D:NOPE_D + LANES]
        qs = q[:, NOPE_D + LANES:]
        ss = jnp.sum(qn * qn + qr * qr, axis=1, keepdims=True)
        rstd = lax.rsqrt(ss * (1.0 / QK_D) + NORM_EPS)
        base = g * HEAD_PAD
        q_ref[:, base:base + NOPE_D] = (qn * rstd * gn_ref[...]).astype(q_ref.dtype)
        roped = qr * cg_s[...] + qs * sg_s[...]
        q_ref[:, base + NOPE_D:base + HEAD_PAD] = (roped * rstd).astype(q_ref.dtype)


def _q_heads(cq, gl, w_uq3, gn, gr, gs, rc, rs, *, tm=1024, heads=8):
    row = lambda i, h: (i, 0)
    vec = lambda i, h: (0, 0)
    return pl.pallas_call(
        functools.partial(_q_heads_kernel, heads=heads),
        grid=(TOKENS // tm, N_HEADS // heads),
        in_specs=[pl.BlockSpec((tm, Q_LORA), row),
                  pl.BlockSpec((1, Q_LORA), vec),
                  pl.BlockSpec((Q_LORA, heads * Q_HEAD_COLS), lambda i, h: (0, h)),
                  pl.BlockSpec((1, LANES), vec),
                  pl.BlockSpec((1, LANES), vec),
                  pl.BlockSpec((1, LANES), vec),
                  pl.BlockSpec((tm, LANES), row),
                  pl.BlockSpec((tm, LANES), row)],
        out_specs=pl.BlockSpec((tm, heads * HEAD_PAD), lambda i, h: (i, h)),
        out_shape=jax.ShapeDtypeStruct((TOKENS, N_HEADS * HEAD_PAD), BF16),
        scratch_shapes=[pltpu.VMEM((tm, Q_LORA), BF16),
                        pltpu.VMEM((tm, LANES), F32),
                        pltpu.VMEM((tm, LANES), F32)],
        compiler_params=_params("parallel", "arbitrary"),
        name="q_heads",
    )(cq, gl.reshape(1, Q_LORA), w_uq3, gn, gr, gs, rc, rs)


def _attn_kernel(q_ref, k_ref, v_ref, wi_ref, wo_ref, o_ref, wi_b_ref, wo_b_ref, m_s, acc_s, *,
                 blk, heads, subs):
    wi_b_ref[...] = wi_ref[...].astype(wi_b_ref.dtype)
    wo_b_ref[...] = wo_ref[...].astype(wo_b_ref.dtype)
    i = pl.program_id(2)
    m_s[...] = jnp.full_like(m_s, NEG_BIG)
    acc_s[...] = jnp.zeros_like(acc_s)

    def chain(g, u, start, masked):
        cols = slice(g * HEAD_PAD, (g + 1) * HEAD_PAD)
        rows = slice(u * blk, (u + 1) * blk)
        k = k_ref[pl.ds(start, blk), cols]
        v = v_ref[pl.ds(start, blk), cols]
        s = lax.dot_general(q_ref[rows, cols], k, (((1,), (1,)), ((), ())),
                            preferred_element_type=F32)
        if masked:
            row = lax.broadcasted_iota(jnp.int32, (blk, blk), 0)
            col = lax.broadcasted_iota(jnp.int32, (blk, blk), 1)
            s = jnp.where(col <= row, s, NEG_BIG)
        m_prev = m_s[g, rows, :]
        m_new = jnp.maximum(m_prev, jnp.max(s, axis=1, keepdims=True))
        p = jnp.exp2(s - jnp.concatenate([m_new] * (blk // LANES), axis=1))
        alpha = jnp.exp2(m_prev - m_new)
        pv = jnp.dot(p.astype(BF16), v, preferred_element_type=F32)
        acc_s[g, rows, :] = (jnp.concatenate([alpha] * (HEAD_PAD // LANES), axis=1)
                             * acc_s[g, rows, :] + pv)
        m_s[g, rows, :] = m_new

    def step(j, active):
        start = pl.multiple_of(j * blk, blk)
        for g in range(heads):
            for u, masked in active:
                chain(g, u, start, masked)

    def body(j, carry):
        step(j, [(u, False) for u in range(subs)])
        return carry

    lax.fori_loop(0, subs * i, body, 0)
    for d in range(subs):
        step(subs * i + d, [(d, True)] + [(u, False) for u in range(d + 1, subs)])
    for g in range(heads):
        acc = acc_s[g]
        o_ref[:, g * V_D:(g + 1) * V_D] = (acc[:, :V_D] / acc[:, V_D:]).astype(o_ref.dtype)


def _attention(q, k, v, ffn_w_in, ffn_w_out, layer, *, blk=512, heads=2, subs=4):
    tq = subs * blk
    nq = SEQ // tq
    nh = N_HEADS // heads
    bands = BATCH * nh
    wi_blk = (D_MODEL // bands, 2 * D_FF // nq)
    wo_blk = (D_FF // bands, D_MODEL // nq)
    tile = lambda b, h, i: (b * nh + h, i)
    src_tile = lambda b, h, i: (layer, b * nh + h, i)
    return pl.pallas_call(
        functools.partial(_attn_kernel, blk=blk, heads=heads, subs=subs),
        grid=(BATCH, nh, nq),
        in_specs=[pl.BlockSpec((tq, heads * HEAD_PAD), lambda b, h, i: (b * nq + i, h)),
                  pl.BlockSpec((SEQ, heads * HEAD_PAD), lambda b, h, i: (b, h)),
                  pl.BlockSpec((SEQ, heads * HEAD_PAD), lambda b, h, i: (b, h)),
                  pl.BlockSpec((None,) + wi_blk, src_tile),
                  pl.BlockSpec((None,) + wo_blk, src_tile)],
        out_specs=[pl.BlockSpec((tq, heads * V_D), lambda b, h, i: (b * nq + i, h)),
                   pl.BlockSpec(wi_blk, tile),
                   pl.BlockSpec(wo_blk, tile)],
        out_shape=[jax.ShapeDtypeStruct((TOKENS, N_HEADS * V_D), BF16),
                   jax.ShapeDtypeStruct((D_MODEL, 2 * D_FF), BF16),
                   jax.ShapeDtypeStruct((D_FF, D_MODEL), BF16)],
        scratch_shapes=[pltpu.VMEM((heads, tq, LANES), F32),
                        pltpu.VMEM((heads, tq, HEAD_PAD), F32)],
        compiler_params=_params("parallel", "parallel", "arbitrary"),
        name="flash_attention",
    )(q, k, v, ffn_w_in, ffn_w_out)


def _swap_halves(t):
    half = ROPE_D // 2
    return jnp.concatenate([t[..., half:], t[..., :half]], axis=-1)


def _split_qk_gain(g, scale):
    pad = lambda v: jnp.pad(v * scale, (0, LANES - ROPE_D)).reshape(1, LANES)
    return (g[:NOPE_D] * scale).reshape(1, LANES), pad(g[NOPE_D:]), pad(_swap_halves(g[NOPE_D:]))


def kernel(x, positions, norm_mix, norm_ffn, lru_w_in, lru_b_in, lru_conv_w, lru_conv_b, lru_w_gate, lru_b_gate, lru_lambda, lru_w_out, lru_b_out, kv_norm_in, w_dkv, kv_latent_norm, w_ukv, k_norm, w_dq, q_latent_norm, w_uq, q_norm, w_o, ffn_w_in, ffn_w_out):
    h = x.reshape(TOKENS, D_MODEL)
    rc, rs = _rope_tables(positions)

    gate, rec_pre, lru_w_out_b, w_o_b = _inproj(h, norm_mix[0], lru_w_in[0].astype(BF16),
                                                lru_b_in[0], lru_w_out, w_o)
    y, ffn_wi0, ffn_wo0 = _rglru(rec_pre, gate, lru_conv_w[0], lru_conv_b[0],
                                 (0.5 * lru_w_gate[0]).astype(BF16), 0.5 * lru_b_gate[0],
                                 lru_lambda[0], ffn_w_in, ffn_w_out, 0)
    h = _linear(y, lru_w_out_b, bias=lru_b_out[0], resid=h, tn=D_MODEL,
                name="lru_out_proj")
    h = _ffn(h, norm_ffn[0], ffn_wi0, ffn_wo0)

    w_dkv_p = jnp.pad(w_dkv, ((0, 0), (0, LANES - ROPE_D))).astype(BF16)
    ckr, cq = _down_proj(h, kv_norm_in, norm_mix[1], w_dkv_p, w_dq[0].astype(BF16))
    kgn, kgr, _ = _split_qk_gain(k_norm, 1.0)
    k_sh, v_sh = _kv_heads(ckr, kv_latent_norm,
                           w_ukv.reshape(KV_LORA, N_HEADS * (NOPE_D + V_D)).astype(BF16),
                           kgn, kgr, rc, rs)

    zpad = jnp.zeros((Q_LORA, N_HEADS, LANES - ROPE_D), F32)
    w_rope = w_uq[0][..., NOPE_D:]
    w_uq3 = jnp.concatenate([w_uq[0][..., :NOPE_D], w_rope, zpad, _swap_halves(w_rope), zpad],
                            axis=-1)
    qgn, qgr, qgs = _split_qk_gain(q_norm[0], math.log2(math.e) / math.sqrt(QK_D))
    q = _q_heads(cq, q_latent_norm[0],
                 w_uq3.reshape(Q_LORA, N_HEADS * Q_HEAD_COLS).astype(BF16),
                 qgn, qgr, qgs, rc, rs)
    o, ffn_wi1, ffn_wo1 = _attention(q, k_sh, v_sh, ffn_w_in, ffn_w_out, 1)
    h = _linear(o, w_o_b, resid=h, tn=D_MODEL, name="attn_out_proj")
    h = _ffn(h, norm_ffn[1], ffn_wi1, ffn_wo1)
    return h.reshape(BATCH, SEQ, D_MODEL)
```

```python
import functools
import math

import jax
import jax.numpy as jnp
from jax import lax
from jax.experimental import pallas as pl
from jax.experimental.pallas import tpu as pltpu

F32 = jnp.float32
BF16 = jnp.bfloat16

D_MODEL = 2048
BATCH = 4
SEQ = 4096
TOKENS = BATCH * SEQ
D_RNN = D_MODEL
LRU_BLOCK = 256
N_BLK = D_RNN // LRU_BLOCK
CONV_W = 4
LRU_C = 8.0
N_HEADS = 16
NOPE_D = 128
ROPE_D = 64
QK_D = NOPE_D + ROPE_D
V_D = 128
Q_LORA = 512
KV_LORA = 512
ROPE_THETA = 10000.0
D_FF = 5632
NORM_EPS = 1e-6

LANES = 128
SUBLANES = 8
HEAD_PAD = 2 * LANES
DOWN_PROJ_CHUNKS = 2
Q_HEAD_COLS = 3 * LANES
LANE_GROUPS = D_RNN // LANES
SCAN_GROUPS = 8
V7X_VMEM_BYTES = 64 * 1024 * 1024
VMEM_LIMIT = V7X_VMEM_BYTES * 7 // 8
NEG_BIG = -1e30


def _params(*semantics):
    return pltpu.CompilerParams(dimension_semantics=semantics, vmem_limit_bytes=VMEM_LIMIT)


def _rms(x, g):
    ms = jnp.mean(x * x, axis=-1, keepdims=True)
    return x * lax.rsqrt(ms + NORM_EPS) * g


def _rope(t, c, s):
    swapped = pltpu.roll(t, LANES - ROPE_D // 2, 1) + pltpu.roll(t, ROPE_D // 2, 1)
    return t * c + swapped * s


def _rope_table_kernel(pos_ref, invf_ref, rc_ref, rs_ref):
    half = ROPE_D // 2
    per_row = LANES // half
    tr = pos_ref.shape[0]
    ang = pos_ref[...].astype(F32) * invf_ref[...]
    cos = jnp.cos(ang)
    sin = jnp.sin(ang)
    low = lax.broadcasted_iota(jnp.int32, (tr, LANES), 1) < half
    for p in range(per_row):
        shift = (LANES - half * p) % LANES
        c = jnp.where(low, pltpu.roll(cos, shift, 1) if shift else cos, 0.0)
        s = jnp.where(low, pltpu.roll(sin, shift, 1) if shift else sin, 0.0)
        rc_ref[pl.ds(p, tr, stride=per_row), :] = c + pltpu.roll(c, half, 1)
        rs_ref[pl.ds(p, tr, stride=per_row), :] = pltpu.roll(s, half, 1) - s


def _rope_tables(positions):
    half = ROPE_D // 2
    per_row = LANES // half
    rows = TOKENS // per_row
    inv_freq = ROPE_THETA ** (-jnp.arange(0, ROPE_D, 2, dtype=F32) / ROPE_D)
    pos = jnp.broadcast_to(positions.reshape(rows, per_row, 1), (rows, per_row, half))
    pos = pos.reshape(rows, LANES)
    invf = jnp.tile(inv_freq, per_row).reshape(1, LANES)
    tr = 512
    return pl.pallas_call(
        _rope_table_kernel,
        grid=(rows // tr,),
        in_specs=[pl.BlockSpec((tr, LANES), lambda i: (i, 0)),
                  pl.BlockSpec((1, LANES), lambda i: (0, 0))],
        out_specs=[pl.BlockSpec((per_row * tr, LANES), lambda i: (i, 0)),
                   pl.BlockSpec((per_row * tr, LANES), lambda i: (i, 0))],
        out_shape=[jax.ShapeDtypeStruct((TOKENS, LANES), F32)] * 2,
        compiler_params=_params("arbitrary"),
        name="rope_tables",
    )(pos, invf)


def _linear_kernel(*refs, has_gain, has_bias, has_resid, stage_x):
    refs = list(refs)
    x_ref = refs.pop(0)
    w_ref = refs.pop(0)
    g_ref = refs.pop(0) if has_gain else None
    b_ref = refs.pop(0) if has_bias else None
    r_ref = refs.pop(0) if has_resid else None
    o_ref = refs.pop(0)
    if stage_x:
        xs_ref = refs.pop(0)

        @pl.when(pl.program_id(1) == 0)
        def _():
            x = x_ref[...].astype(F32)
            if has_gain:
                x = _rms(x, g_ref[...])
            xs_ref[...] = x.astype(BF16)

        xv = xs_ref[...]
    else:
        xv = x_ref[...]
    acc = jnp.dot(xv, w_ref[...], preferred_element_type=F32)
    if has_bias:
        acc = acc + b_ref[...]
    if has_resid:
        acc = acc + r_ref[...]
    o_ref[...] = acc.astype(o_ref.dtype)


def _linear(x, w, *, gain=None, bias=None, resid=None, out_dtype=F32, tm=512, tn=1024, name):
    m, k = x.shape
    n = w.shape[1]
    tn = min(tn, n)
    stage_x = gain is not None or x.dtype != BF16
    in_specs = [pl.BlockSpec((tm, k), lambda i, j: (i, 0)),
                pl.BlockSpec((k, tn), lambda i, j: (0, j))]
    args = [x, w]
    if gain is not None:
        in_specs.append(pl.BlockSpec((1, k), lambda i, j: (0, 0)))
        args.append(gain.reshape(1, k))
    if bias is not None:
        in_specs.append(pl.BlockSpec((1, tn), lambda i, j: (0, j)))
        args.append(bias.reshape(1, n))
    if resid is not None:
        in_specs.append(pl.BlockSpec((tm, tn), lambda i, j: (i, j)))
        args.append(resid)
    scratch = [pltpu.VMEM((tm, k), BF16)] if stage_x else []
    return pl.pallas_call(
        functools.partial(_linear_kernel, has_gain=gain is not None, has_bias=bias is not None,
                          has_resid=resid is not None, stage_x=stage_x),
        grid=(m // tm, n // tn),
        in_specs=in_specs,
        out_specs=pl.BlockSpec((tm, tn), lambda i, j: (i, j)),
        out_shape=jax.ShapeDtypeStruct((m, n), out_dtype),
        scratch_shapes=scratch,
        compiler_params=_params("parallel", "arbitrary"),
        name=name,
    )(*args)


def _down_proj_kernel(x_ref, gkv_ref, gq_ref, wkv_ref, wq_ref, ckr_ref, cq_ref):
    chunk = x_ref.shape[0] // DOWN_PROJ_CHUNKS
    for r in range(DOWN_PROJ_CHUNKS):
        rows = slice(r * chunk, (r + 1) * chunk)
        x = x_ref[rows, :]
        xhat = x * lax.rsqrt(jnp.mean(x * x, axis=-1, keepdims=True) + NORM_EPS)
        ckr_ref[rows, :] = jnp.dot((xhat * gkv_ref[...]).astype(BF16), wkv_ref[...],
                                   preferred_element_type=F32)
        cq_ref[rows, :] = jnp.dot((xhat * gq_ref[...]).astype(BF16), wq_ref[...],
                                  preferred_element_type=F32)


def _down_proj(x, g_kv, g_q, w_dkv, w_dq, *, tm=1024):
    n_kv = w_dkv.shape[1]
    row = lambda i: (i, 0)
    full = lambda i: (0, 0)
    return pl.pallas_call(
        _down_proj_kernel,
        grid=(TOKENS // tm,),
        in_specs=[pl.BlockSpec((tm, D_MODEL), row),
                  pl.BlockSpec((1, D_MODEL), full),
                  pl.BlockSpec((1, D_MODEL), full),
                  pl.BlockSpec((D_MODEL, n_kv), full),
                  pl.BlockSpec((D_MODEL, Q_LORA), full)],
        out_specs=[pl.BlockSpec((tm, n_kv), row),
                   pl.BlockSpec((tm, Q_LORA), row)],
        out_shape=[jax.ShapeDtypeStruct((TOKENS, n_kv), F32),
                   jax.ShapeDtypeStruct((TOKENS, Q_LORA), F32)],
        compiler_params=_params("parallel"),
        name="mla_down_proj",
    )(x, g_kv.reshape(1, D_MODEL), g_q.reshape(1, D_MODEL), w_dkv, w_dq)


def _gelu_tanh(x):
    k = -2.0 * math.sqrt(2.0 / math.pi) * math.log2(math.e)
    e = jnp.exp2(x * (k + (k * 0.044715) * (x * x)))
    return x / (1.0 + e)


def _inproj_kernel(x_ref, g_ref, wa_ref, wb_ref, ba_ref, bb_ref, w1_ref, w2_ref,
                   gate_ref, rec_ref, w1_b_ref, w2_b_ref, xs_ref):
    w1_b_ref[...] = w1_ref[...].astype(w1_b_ref.dtype)
    w2_b_ref[...] = w2_ref[...].astype(w2_b_ref.dtype)

    def branch(xv):
        ga = jnp.dot(xv, wa_ref[...], preferred_element_type=F32) + ba_ref[...]
        gate_ref[...] = _gelu_tanh(ga).astype(gate_ref.dtype)
        rec_ref[...] = jnp.dot(xv, wb_ref[...], preferred_element_type=F32) + bb_ref[...]

    @pl.when(pl.program_id(1) == 0)
    def _():
        xv = _rms(x_ref[...], g_ref[...]).astype(BF16)
        xs_ref[...] = xv
        branch(xv)

    @pl.when(pl.program_id(1) != 0)
    def _():
        branch(xs_ref[...])


def _inproj(x, gain, w_in, b_in, sq1, sq2, *, tm=512, tn=1024):
    ni = TOKENS // tm
    nj = D_RNN // tn
    b2 = b_in.reshape(1, 2 * D_RNN)
    sq_blk = (D_MODEL // ni, D_MODEL // nj)
    sq_in = pl.BlockSpec((None,) + sq_blk, lambda i, j: (0, i, j))
    sq_out = pl.BlockSpec(sq_blk, lambda i, j: (i, j))
    sq_shape = jax.ShapeDtypeStruct((D_MODEL, D_MODEL), BF16)
    return pl.pallas_call(
        _inproj_kernel,
        grid=(ni, nj),
        in_specs=[pl.BlockSpec((tm, D_MODEL), lambda i, j: (i, 0)),
                  pl.BlockSpec((1, D_MODEL), lambda i, j: (0, 0)),
                  pl.BlockSpec((D_MODEL, tn), lambda i, j: (0, j)),
                  pl.BlockSpec((D_MODEL, tn), lambda i, j: (0, j + nj)),
                  pl.BlockSpec((1, tn), lambda i, j: (0, j)),
                  pl.BlockSpec((1, tn), lambda i, j: (0, j + nj)),
                  sq_in, sq_in],
        out_specs=[pl.BlockSpec((tm, tn), lambda i, j: (i, j)),
                   pl.BlockSpec((tm, tn), lambda i, j: (i, j)),
                   sq_out, sq_out],
        out_shape=[jax.ShapeDtypeStruct((TOKENS, D_RNN), BF16),
                   jax.ShapeDtypeStruct((TOKENS, D_RNN), F32),
                   sq_shape, sq_shape],
        scratch_shapes=[pltpu.VMEM((tm, D_MODEL), BF16)],
        compiler_params=_params("parallel", "arbitrary"),
        name="lru_in_proj",
    )(x, gain.reshape(1, D_MODEL), w_in, w_in, b2, b2, sq1, sq2)


def _rglru_kernel(x_ref, gate_ref, cw_ref, cb_ref, wg_ref, bg_ref, lam_ref, wi_ref, wo_ref,
                  y_ref, wi_b_ref, wo_b_ref,
                  xp_s, a_s, b_s, tail_s, eq_s, cm_s, hc_s, *, chunk):
    wi_b_ref[...] = wi_ref[...].astype(wi_b_ref.dtype)
    wo_b_ref[...] = wo_ref[...].astype(wo_b_ref.dtype)

    c = pl.program_id(1)
    seg = chunk // SUBLANES
    halo = SUBLANES * (CONV_W - 1)

    @pl.when(c == 0)
    def _():
        tail_s[...] = jnp.zeros_like(tail_s)
        hc_s[...] = jnp.zeros_like(hc_s)

    for g in range(LANE_GROUPS):
        lanes = slice(g * LANES, (g + 1) * LANES)
        for j in range(SUBLANES):
            xp_s[g, pl.ds(halo + j, seg, stride=SUBLANES), :] = x_ref[j * seg:(j + 1) * seg, lanes]

    first_sublane = lax.broadcasted_iota(jnp.int32, (SUBLANES, LANES), 0) == 0
    for g in range(LANE_GROUPS):
        for d in range(1, CONV_W):
            slot = slice(SUBLANES * (CONV_W - 1 - d), SUBLANES * (CONV_W - d))
            cur = xp_s[g, halo + SUBLANES * (seg - d):halo + SUBLANES * (seg - d + 1), :]
            prev = tail_s[g, slot, :]
            xp_s[g, slot, :] = jnp.where(first_sublane, pltpu.roll(prev, 1, 0),
                                         pltpu.roll(cur, 1, 0))
            tail_s[g, slot, :] = cur

    lam = lam_ref[...]
    sp = jnp.maximum(-lam, 0.0) + jnp.log1p(jnp.exp(-jnp.abs(lam)))
    neg_half_c_sp = (-0.5 * LRU_C) * sp
    per_blk = LRU_BLOCK // LANES
    for n in range(N_BLK):
        parts = []
        for sub in range(per_blk):
            g = n * per_blk + sub
            lanes = slice(g * LANES, (g + 1) * LANES)
            acc = cb_ref[:, lanes] + cw_ref[CONV_W - 1:CONV_W, lanes] * xp_s[g, halo:halo + chunk, :]
            for d in range(1, CONV_W):
                off = halo - SUBLANES * d
                acc = acc + cw_ref[CONV_W - 1 - d:CONV_W - d, lanes] * xp_s[g, off:off + chunk, :]
            parts.append(acc)
        xn = jnp.concatenate(parts, axis=1)
        cols = slice(n * LRU_BLOCK, (n + 1) * LRU_BLOCK)
        gh = jnp.dot(xn.astype(BF16), wg_ref[n], preferred_element_type=F32) + bg_ref[n:n + 1, :]
        tr = jnp.tanh(gh[:, :LRU_BLOCK])
        i = 0.5 * jnp.tanh(gh[:, LRU_BLOCK:]) + 0.5
        log_a = (tr + 1.0) * neg_half_c_sp[:, cols]
        a = jnp.exp(log_a)
        x1 = -jnp.tanh(log_a) * (a * a + 1.0)
        mult = jnp.where(x1 > 0.0, x1 * lax.rsqrt(x1), 0.0)
        b = mult * (i * xn)
        for sub in range(per_blk):
            lanes = slice(sub * LANES, (sub + 1) * LANES)
            a_s[n * per_blk + sub] = a[:, lanes]
            b_s[n * per_blk + sub] = b[:, lanes]

    def tile(ref, g, s):
        return ref[g, pl.ds(pl.multiple_of(s * SUBLANES, SUBLANES), SUBLANES), :]

    for g0 in range(0, LANE_GROUPS, SCAN_GROUPS):
        groups = range(g0, g0 + SCAN_GROUPS)

        def pass1(s, carry):
            hs, ps = carry
            a = [tile(a_s, g, s) for g in groups]
            b = [tile(b_s, g, s) for g in groups]
            return (tuple(ai * hi + bi for ai, hi, bi in zip(a, hs, b)),
                    tuple(ai * pi for ai, pi in zip(a, ps)))

        zero = jnp.zeros((SUBLANES, LANES), F32)
        one = jnp.ones((SUBLANES, LANES), F32)
        e, q = lax.fori_loop(0, seg, pass1, ((zero,) * SCAN_GROUPS, (one,) * SCAN_GROUPS),
                             unroll=8)
        for idx, g in enumerate(groups):
            eq_s[0] = e[idx]
            eq_s[1] = q[idx]
            h = hc_s[g, 0:1, :]
            for j in range(SUBLANES):
                cm_s[g, j:j + 1, :] = h
                h = eq_s[1, j:j + 1, :] * h + eq_s[0, j:j + 1, :]
            hc_s[g, 0:1, :] = h

        def pass2(s, hs):
            out = []
            for g, h in zip(groups, hs):
                h = tile(a_s, g, s) * h + tile(b_s, g, s)
                b_s[g, pl.ds(pl.multiple_of(s * SUBLANES, SUBLANES), SUBLANES), :] = h
                out.append(h)
            return tuple(out)

        lax.fori_loop(0, seg, pass2, tuple(cm_s[g] for g in groups), unroll=8)

    for g in range(LANE_GROUPS):
        lanes = slice(g * LANES, (g + 1) * LANES)
        for j in range(SUBLANES):
            rows = slice(j * seg, (j + 1) * seg)
            h = b_s[g, pl.ds(j, seg, stride=SUBLANES), :]
            y_ref[rows, lanes] = gate_ref[rows, lanes] * h.astype(y_ref.dtype)


def _rglru(rec_pre, gate, conv_w, conv_b, w_gate, b_gate, lam, ffn_w_in, ffn_w_out, layer, *,
           chunk=512):
    nc = SEQ // chunk
    steps = BATCH * nc
    wi_rows = D_MODEL // steps
    wo_rows = D_FF // steps
    row = lambda b, c: (b * nc + c, 0)
    band = lambda b, c: (layer, b * nc + c, 0)
    full2 = lambda b, c: (0, 0)
    return pl.pallas_call(
        functools.partial(_rglru_kernel, chunk=chunk),
        grid=(BATCH, nc),
        in_specs=[pl.BlockSpec((chunk, D_RNN), row),
                  pl.BlockSpec((chunk, D_RNN), row),
                  pl.BlockSpec((CONV_W, D_RNN), full2),
                  pl.BlockSpec((1, D_RNN), full2),
                  pl.BlockSpec((N_BLK, LRU_BLOCK, 2 * LRU_BLOCK), lambda b, c: (0, 0, 0)),
                  pl.BlockSpec((N_BLK, 2 * LRU_BLOCK), full2),
                  pl.BlockSpec((1, D_RNN), full2),
                  pl.BlockSpec((None, wi_rows, 2 * D_FF), band),
                  pl.BlockSpec((None, wo_rows, D_MODEL), band)],
        out_specs=[pl.BlockSpec((chunk, D_RNN), row),
                   pl.BlockSpec((wi_rows, 2 * D_FF), row),
                   pl.BlockSpec((wo_rows, D_MODEL), row)],
        out_shape=[jax.ShapeDtypeStruct((TOKENS, D_RNN), BF16),
                   jax.ShapeDtypeStruct((D_MODEL, 2 * D_FF), BF16),
                   jax.ShapeDtypeStruct((D_FF, D_MODEL), BF16)],
        scratch_shapes=[pltpu.VMEM((LANE_GROUPS, SUBLANES * (CONV_W - 1) + chunk, LANES), F32),
                        pltpu.VMEM((LANE_GROUPS, chunk, LANES), F32),
                        pltpu.VMEM((LANE_GROUPS, chunk, LANES), F32),
                        pltpu.VMEM((LANE_GROUPS, SUBLANES * (CONV_W - 1), LANES), F32),
                        pltpu.VMEM((2, SUBLANES, LANES), F32),
                        pltpu.VMEM((LANE_GROUPS, SUBLANES, LANES), F32),
                        pltpu.VMEM((LANE_GROUPS, SUBLANES, LANES), F32)],
        compiler_params=_params("parallel", "arbitrary"),
        name="rglru_core",
    )(rec_pre, gate, conv_w, conv_b.reshape(1, D_RNN), w_gate, b_gate, lam.reshape(1, D_RNN),
      ffn_w_in, ffn_w_out)


def _ffn_kernel(h_ref, g_ref, wa_ref, wb_ref, w2_ref, o_ref, hn_s):
    f = pl.program_id(1)

    def branch(hn):
        ua = jnp.dot(hn, wa_ref[...], preferred_element_type=F32)
        ub = jnp.dot(hn, wb_ref[...], preferred_element_type=F32)
        act = (jax.nn.silu(ua) * ub).astype(BF16)
        return jnp.dot(act, w2_ref[...], preferred_element_type=F32)

    @pl.when(f == 0)
    def _():
        h = h_ref[...]
        hn = _rms(h, g_ref[...]).astype(BF16)
        hn_s[...] = hn
        o_ref[...] = h + branch(hn)

    @pl.when(f != 0)
    def _():
        o_ref[...] += branch(hn_s[...])


def _ffn(h, gain, w_in, w_out, *, tm=1024, tf=512):
    nf = D_FF // tf
    return pl.pallas_call(
        _ffn_kernel,
        grid=(TOKENS // tm, nf),
        in_specs=[pl.BlockSpec((tm, D_MODEL), lambda i, f: (i, 0)),
                  pl.BlockSpec((1, D_MODEL), lambda i, f: (0, 0)),
                  pl.BlockSpec((D_MODEL, tf), lambda i, f: (0, f)),
                  pl.BlockSpec((D_MODEL, tf), lambda i, f: (0, f + nf)),
                  pl.BlockSpec((tf, D_MODEL), lambda i, f: (f, 0))],
        out_specs=pl.BlockSpec((tm, D_MODEL), lambda i, f: (i, 0)),
        out_shape=jax.ShapeDtypeStruct((TOKENS, D_MODEL), F32),
        scratch_shapes=[pltpu.VMEM((tm, D_MODEL), BF16)],
        compiler_params=_params("parallel", "arbitrary"),
        name="swiglu_ffn",
    )(h, gain.reshape(1, D_MODEL), w_in, w_in, w_out)


def _kv_heads_kernel(c_ref, kr_ref, gl_ref, w_ref, gn_ref, gr_ref, rc_ref, rs_ref,
                     k_ref, v_ref, cn_s, krr_s, ssr_s, *, heads):
    @pl.when(pl.program_id(1) == 0)
    def _():
        cn_s[...] = _rms(c_ref[...], gl_ref[...]).astype(BF16)
        kr = kr_ref[...]
        ssr_s[...] = jnp.broadcast_to(jnp.sum(kr * kr, axis=1, keepdims=True), ssr_s.shape)
        krr_s[...] = _rope(kr * gr_ref[...], rc_ref[...], rs_ref[...])

    ones = jnp.ones((v_ref.shape[0], HEAD_PAD - V_D), v_ref.dtype)
    kv_all = jnp.dot(cn_s[...], w_ref[...], preferred_element_type=F32)
    for g in range(heads):
        base = g * HEAD_PAD
        kv = kv_all[:, base:base + HEAD_PAD]
        kn = kv[:, :NOPE_D]
        ss = jnp.sum(kn * kn, axis=1, keepdims=True) + ssr_s[...]
        rstd = lax.rsqrt(ss * (1.0 / QK_D) + NORM_EPS)
        k_ref[:, base:base + NOPE_D] = (kn * rstd * gn_ref[...]).astype(k_ref.dtype)
        k_ref[:, base + NOPE_D:base + HEAD_PAD] = (krr_s[...] * rstd).astype(k_ref.dtype)
        v_ref[:, base:base + V_D] = kv[:, NOPE_D:].astype(v_ref.dtype)
        v_ref[:, base + V_D:base + HEAD_PAD] = ones


def _kv_heads(ckr, gl, w_ukv, gn, gr, rc, rs, *, tm=1024, heads=8):
    rope_blk = KV_LORA // LANES
    row = lambda i, h: (i, 0)
    vec = lambda i, h: (0, 0)
    return pl.pallas_call(
        functools.partial(_kv_heads_kernel, heads=heads),
        grid=(TOKENS // tm, N_HEADS // heads),
        in_specs=[pl.BlockSpec((tm, KV_LORA), row),
                  pl.BlockSpec((tm, LANES), lambda i, h: (i, rope_blk)),
                  pl.BlockSpec((1, KV_LORA), vec),
                  pl.BlockSpec((KV_LORA, heads * HEAD_PAD), lambda i, h: (0, h)),
                  pl.BlockSpec((1, LANES), vec),
                  pl.BlockSpec((1, LANES), vec),
                  pl.BlockSpec((tm, LANES), row),
                  pl.BlockSpec((tm, LANES), row)],
        out_specs=[pl.BlockSpec((tm, heads * HEAD_PAD), lambda i, h: (i, h)),
                   pl.BlockSpec((tm, heads * HEAD_PAD), lambda i, h: (i, h))],
        out_shape=[jax.ShapeDtypeStruct((TOKENS, N_HEADS * HEAD_PAD), BF16),
                   jax.ShapeDtypeStruct((TOKENS, N_HEADS * HEAD_PAD), BF16)],
        scratch_shapes=[pltpu.VMEM((tm, KV_LORA), BF16),
                        pltpu.VMEM((tm, LANES), F32),
                        pltpu.VMEM((tm, LANES), F32)],
        compiler_params=_params("parallel", "arbitrary"),
        name="kv_heads",
    )(ckr, ckr, gl.reshape(1, KV_LORA), w_ukv, gn, gr, rc, rs)


def _q_heads_kernel(c_ref, gl_ref, w_ref, gn_ref, gr_ref, gs_ref, rc_ref, rs_ref,
                    q_ref, cn_s, cg_s, sg_s, *, heads):
    @pl.when(pl.program_id(1) == 0)
    def _():
        cn_s[...] = _rms(c_ref[...], gl_ref[...]).astype(BF16)
        cg_s[...] = rc_ref[...] * gr_ref[...]
        sg_s[...] = rs_ref[...] * gs_ref[...]

    q_all = jnp.dot(cn_s[...], w_ref[...], preferred_element_type=F32)
    for g in range(heads):
        q = q_all[:, g * Q_HEAD_COLS:(g + 1) * Q_HEAD_COLS]
        qn = q[:, :NOPE_D]
        qr = q[:, NOPE_D:NOPE_D + LANES]
        qs = q[:, NOPE_D + LANES:]
        ss = jnp.sum(qn * qn + qr * qr, axis=1, keepdims=True)
        rstd = lax.rsqrt(ss * (1.0 / QK_D) + NORM_EPS)
        base = g * HEAD_PAD
        q_ref[:, base:base + NOPE_D] = (qn * rstd * gn_ref[...]).astype(q_ref.dtype)
        roped = qr * cg_s[...] + qs * sg_s[...]
        q_ref[:, base + NOPE_D:base + HEAD_PAD] = (roped * rstd).astype(q_ref.dtype)


def _q_heads(cq, gl, w_uq3, gn, gr, gs, rc, rs, *, tm=1024, heads=8):
    row = lambda i, h: (i, 0)
    vec = lambda i, h: (0, 0)
    return pl.pallas_call(
        functools.partial(_q_heads_kernel, heads=heads),
        grid=(TOKENS // tm, N_HEADS // heads),
        in_specs=[pl.BlockSpec((tm, Q_LORA), row),
                  pl.BlockSpec((1, Q_LORA), vec),
                  pl.BlockSpec((Q_LORA, heads * Q_HEAD_COLS), lambda i, h: (0, h)),
                  pl.BlockSpec((1, LANES), vec),
                  pl.BlockSpec((1, LANES), vec),
                  pl.BlockSpec((1, LANES), vec),
                  pl.BlockSpec((tm, LANES), row),
                  pl.BlockSpec((tm, LANES), row)],
        out_specs=pl.BlockSpec((tm, heads * HEAD_PAD), lambda i, h: (i, h)),
        out_shape=jax.ShapeDtypeStruct((TOKENS, N_HEADS * HEAD_PAD), BF16),
        scratch_shapes=[pltpu.VMEM((tm, Q_LORA), BF16),
                        pltpu.VMEM((tm, LANES), F32),
                        pltpu.VMEM((tm, LANES), F32)],
        compiler_params=_params("parallel", "arbitrary"),
        name="q_heads",
    )(cq, gl.reshape(1, Q_LORA), w_uq3, gn, gr, gs, rc, rs)


def _attn_kernel(q_ref, k_ref, v_ref, wi_ref, wo_ref, o_ref, wi_b_ref, wo_b_ref, m_s, acc_s, *,
                 blk, heads, subs):
    wi_b_ref[...] = wi_ref[...].astype(wi_b_ref.dtype)
    wo_b_ref[...] = wo_ref[...].astype(wo_b_ref.dtype)
    i = pl.program_id(2)
    m_s[...] = jnp.full_like(m_s, NEG_BIG)
    acc_s[...] = jnp.zeros_like(acc_s)

    def chain(g, u, start, masked):
        cols = slice(g * HEAD_PAD, (g + 1) * HEAD_PAD)
        rows = slice(u * blk, (u + 1) * blk)
        k = k_ref[pl.ds(start, blk), cols]
        v = v_ref[pl.ds(start, blk), cols]
        s = lax.dot_general(q_ref[rows, cols], k, (((1,), (1,)), ((), ())),
                            preferred_element_type=F32)
        if masked:
            row = lax.broadcasted_iota(jnp.int32, (blk, blk), 0)
            col = lax.broadcasted_iota(jnp.int32, (blk, blk), 1)
            s = jnp.where(col <= row, s, NEG_BIG)
        m_prev = m_s[g, rows, :]
        m_new = jnp.maximum(m_prev, jnp.max(s, axis=1, keepdims=True))
        p = jnp.exp2(s - jnp.concatenate([m_new] * (blk // LANES), axis=1))
        alpha = jnp.exp2(m_prev - m_new)
        pv = jnp.dot(p.astype(BF16), v, preferred_element_type=F32)
        acc_s[g, rows, :] = (jnp.concatenate([alpha] * (HEAD_PAD // LANES), axis=1)
                             * acc_s[g, rows, :] + pv)
        m_s[g, rows, :] = m_new

    def step(j, active):
        start = pl.multiple_of(j * blk, blk)
        for g in range(heads):
            for u, masked in active:
                chain(g, u, start, masked)

    def body(j, carry):
        step(j, [(u, False) for u in range(subs)])
        return carry

    lax.fori_loop(0, subs * i, body, 0)
    for d in range(subs):
        step(subs * i + d, [(d, True)] + [(u, False) for u in range(d + 1, subs)])
    for g in range(heads):
        acc = acc_s[g]
        o_ref[:, g * V_D:(g + 1) * V_D] = (acc[:, :V_D] / acc[:, V_D:]).astype(o_ref.dtype)


def _attention(q, k, v, ffn_w_in, ffn_w_out, layer, *, blk=512, heads=2, subs=4):
    tq = subs * blk
    nq = SEQ // tq
    nh = N_HEADS // heads
    bands = BATCH * nh
    wi_blk = (D_MODEL // bands, 2 * D_FF // nq)
    wo_blk = (D_FF // bands, D_MODEL // nq)
    tile = lambda b, h, i: (b * nh + h, i)
    src_tile = lambda b, h, i: (layer, b * nh + h, i)
    return pl.pallas_call(
        functools.partial(_attn_kernel, blk=blk, heads=heads, subs=subs),
        grid=(BATCH, nh, nq),
        in_specs=[pl.BlockSpec((tq, heads * HEAD_PAD), lambda b, h, i: (b * nq + i, h)),
                  pl.BlockSpec((SEQ, heads * HEAD_PAD), lambda b, h, i: (b, h)),
                  pl.BlockSpec((SEQ, heads * HEAD_PAD), lambda b, h, i: (b, h)),
                  pl.BlockSpec((None,) + wi_blk, src_tile),
                  pl.BlockSpec((None,) + wo_blk, src_tile)],
        out_specs=[pl.BlockSpec((tq, heads * V_D), lambda b, h, i: (b * nq + i, h)),
                   pl.BlockSpec(wi_blk, tile),
                   pl.BlockSpec(wo_blk, tile)],
        out_shape=[jax.ShapeDtypeStruct((TOKENS, N_HEADS * V_D), BF16),
                   jax.ShapeDtypeStruct((D_MODEL, 2 * D_FF), BF16),
                   jax.ShapeDtypeStruct((D_FF, D_MODEL), BF16)],
        scratch_shapes=[pltpu.VMEM((heads, tq, LANES), F32),
                        pltpu.VMEM((heads, tq, HEAD_PAD), F32)],
        compiler_params=_params("parallel", "parallel", "arbitrary"),
        name="flash_attention",
    )(q, k, v, ffn_w_in, ffn_w_out)


def _swap_halves(t):
    half = ROPE_D // 2
    return jnp.concatenate([t[..., half:], t[..., :half]], axis=-1)


def _split_qk_gain(g, scale):
    pad = lambda v: jnp.pad(v * scale, (0, LANES - ROPE_D)).reshape(1, LANES)
    return (g[:NOPE_D] * scale).reshape(1, LANES), pad(g[NOPE_D:]), pad(_swap_halves(g[NOPE_D:]))


def kernel(x, positions, norm_mix, norm_ffn, lru_w_in, lru_b_in, lru_conv_w, lru_conv_b, lru_w_gate, lru_b_gate, lru_lambda, lru_w_out, lru_b_out, kv_norm_in, w_dkv, kv_latent_norm, w_ukv, k_norm, w_dq, q_latent_norm, w_uq, q_norm, w_o, ffn_w_in, ffn_w_out):
    h = x.reshape(TOKENS, D_MODEL)
    rc, rs = _rope_tables(positions)

    gate, rec_pre, lru_w_out_b, w_o_b = _inproj(h, norm_mix[0], lru_w_in[0].astype(BF16),
                                                lru_b_in[0], lru_w_out, w_o)
    y, ffn_wi0, ffn_wo0 = _rglru(rec_pre, gate, lru_conv_w[0], lru_conv_b[0],
                                 (0.5 * lru_w_gate[0]).astype(BF16), 0.5 * lru_b_gate[0],
                                 lru_lambda[0], ffn_w_in, ffn_w_out, 0)
    h = _linear(y, lru_w_out_b, bias=lru_b_out[0], resid=h, tn=D_MODEL,
                name="lru_out_proj")
    h = _ffn(h, norm_ffn[0], ffn_wi0, ffn_wo0)

    w_dkv_p = jnp.pad(w_dkv, ((0, 0), (0, LANES - ROPE_D))).astype(BF16)
    ckr, cq = _down_proj(h, kv_norm_in, norm_mix[1], w_dkv_p, w_dq[0].astype(BF16))
    kgn, kgr, _ = _split_qk_gain(k_norm, 1.0)
    k_sh, v_sh = _kv_heads(ckr, kv_latent_norm,
                           w_ukv.reshape(KV_LORA, N_HEADS * (NOPE_D + V_D)).astype(BF16),
                           kgn, kgr, rc, rs)

    zpad = jnp.zeros((Q_LORA, N_HEADS, LANES - ROPE_D), F32)
    w_rope = w_uq[0][..., NOPE_D:]
    w_uq3 = jnp.concatenate([w_uq[0][..., :NOPE_D], w_rope, zpad, _swap_halves(w_rope), zpad],
                            axis=-1)
    qgn, qgr, qgs = _split_qk_gain(q_norm[0], math.log2(math.e) / math.sqrt(QK_D))
    q = _q_heads(cq, q_latent_norm[0],
                 w_uq3.reshape(Q_LORA, N_HEADS * Q_HEAD_COLS).astype(BF16),
                 qgn, qgr, qgs, rc, rs)
    o, ffn_wi1, ffn_wo1 = _attention(q, k_sh, v_sh, ffn_w_in, ffn_w_out, 1)
    h = _linear(o, w_o_b, resid=h, tn=D_MODEL, name="attn_out_proj")
    h = _ffn(h, norm_ffn[1], ffn_wi1, ffn_wo1)
    return h.reshape(BATCH, SEQ, D_MODEL)
```

```python
import functools
import math

import jax
import jax.numpy as jnp
from jax import lax
from jax.experimental import pallas as pl
from jax.experimental.pallas import tpu as pltpu

F32 = jnp.float32
BF16 = jnp.bfloat16

D_MODEL = 2048
BATCH = 4
SEQ = 4096
TOKENS = BATCH * SEQ
D_RNN = D_MODEL
LRU_BLOCK = 256
N_BLK = D_RNN // LRU_BLOCK
CONV_W = 4
LRU_C = 8.0
N_HEADS = 16
NOPE_D = 128
ROPE_D = 64
QK_D = NOPE_D + ROPE_D
V_D = 128
Q_LORA = 512
KV_LORA = 512
ROPE_THETA = 10000.0
D_FF = 5632
NORM_EPS = 1e-6

LANES = 128
SUBLANES = 8
HEAD_PAD = 2 * LANES
ROPE_T2_LANE = LANES // 2
Q_HEAD_COLS = 3 * LANES
KV_UNROLL = 4
DOWN_PROJ_CHUNKS = 2
LANE_GROUPS = D_RNN // LANES
SCAN_GROUPS = 8
V7X_VMEM_BYTES = 64 * 1024 * 1024
VMEM_LIMIT = V7X_VMEM_BYTES * 7 // 8
NEG_BIG = -1e30


def _params(*semantics):
    return pltpu.CompilerParams(dimension_semantics=semantics, vmem_limit_bytes=VMEM_LIMIT)


def _rms(x, g):
    ms = jnp.mean(x * x, axis=-1, keepdims=True)
    return x * lax.rsqrt(ms + NORM_EPS) * g


def _rope(t, c, s):
    return t * c + pltpu.roll(t, ROPE_T2_LANE, 1) * s


def _spread_rope(v):
    half = ROPE_D // 2
    z = jnp.zeros(v.shape[:-1] + (ROPE_T2_LANE - half,), v.dtype)
    return jnp.concatenate([v[..., :half], z, v[..., half:], z], axis=-1)


def _rope_table_kernel(pos_ref, invf_ref, rc_ref, rs_ref):
    half = ROPE_D // 2
    per_row = LANES // half
    tr = pos_ref.shape[0]
    ang = pos_ref[...].astype(F32) * invf_ref[...]
    cos = jnp.cos(ang)
    sin = jnp.sin(ang)
    low = lax.broadcasted_iota(jnp.int32, (tr, LANES), 1) < half
    for p in range(per_row):
        shift = (LANES - half * p) % LANES
        c = jnp.where(low, pltpu.roll(cos, shift, 1) if shift else cos, 0.0)
        s = jnp.where(low, pltpu.roll(sin, shift, 1) if shift else sin, 0.0)
        rc_ref[pl.ds(p, tr, stride=per_row), :] = c + pltpu.roll(c, ROPE_T2_LANE, 1)
        rs_ref[pl.ds(p, tr, stride=per_row), :] = pltpu.roll(s, ROPE_T2_LANE, 1) - s


def _rope_tables(positions):
    half = ROPE_D // 2
    per_row = LANES // half
    rows = TOKENS // per_row
    inv_freq = ROPE_THETA ** (-jnp.arange(0, ROPE_D, 2, dtype=F32) / ROPE_D)
    pos = jnp.broadcast_to(positions.reshape(rows, per_row, 1), (rows, per_row, half))
    pos = pos.reshape(rows, LANES)
    invf = jnp.tile(inv_freq, per_row).reshape(1, LANES)
    tr = 512
    return pl.pallas_call(
        _rope_table_kernel,
        grid=(rows // tr,),
        in_specs=[pl.BlockSpec((tr, LANES), lambda i: (i, 0)),
                  pl.BlockSpec((1, LANES), lambda i: (0, 0))],
        out_specs=[pl.BlockSpec((per_row * tr, LANES), lambda i: (i, 0)),
                   pl.BlockSpec((per_row * tr, LANES), lambda i: (i, 0))],
        out_shape=[jax.ShapeDtypeStruct((TOKENS, LANES), F32)] * 2,
        compiler_params=_params("arbitrary"),
        name="rope_tables",
    )(pos, invf)


def _linear_kernel(*refs, has_gain, has_bias, has_resid, stage_x):
    refs = list(refs)
    x_ref = refs.pop(0)
    w_ref = refs.pop(0)
    g_ref = refs.pop(0) if has_gain else None
    b_ref = refs.pop(0) if has_bias else None
    r_ref = refs.pop(0) if has_resid else None
    o_ref = refs.pop(0)
    if stage_x:
        xs_ref = refs.pop(0)

        @pl.when(pl.program_id(1) == 0)
        def _():
            x = x_ref[...].astype(F32)
            if has_gain:
                x = _rms(x, g_ref[...])
            xs_ref[...] = x.astype(BF16)

        xv = xs_ref[...]
    else:
        xv = x_ref[...]
    acc = jnp.dot(xv, w_ref[...], preferred_element_type=F32)
    if has_bias:
        acc = acc + b_ref[...]
    if has_resid:
        acc = acc + r_ref[...]
    o_ref[...] = acc.astype(o_ref.dtype)


def _linear(x, w, *, gain=None, bias=None, resid=None, out_dtype=F32, tm=512, tn=1024, name):
    m, k = x.shape
    n = w.shape[1]
    tn = min(tn, n)
    stage_x = gain is not None or x.dtype != BF16
    in_specs = [pl.BlockSpec((tm, k), lambda i, j: (i, 0)),
                pl.BlockSpec((k, tn), lambda i, j: (0, j))]
    args = [x, w]
    if gain is not None:
        in_specs.append(pl.BlockSpec((1, k), lambda i, j: (0, 0)))
        args.append(gain.reshape(1, k))
    if bias is not None:
        in_specs.append(pl.BlockSpec((1, tn), lambda i, j: (0, j)))
        args.append(bias.reshape(1, n))
    if resid is not None:
        in_specs.append(pl.BlockSpec((tm, tn), lambda i, j: (i, j)))
        args.append(resid)
    scratch = [pltpu.VMEM((tm, k), BF16)] if stage_x else []
    return pl.pallas_call(
        functools.partial(_linear_kernel, has_gain=gain is not None, has_bias=bias is not None,
                          has_resid=resid is not None, stage_x=stage_x),
        grid=(m // tm, n // tn),
        in_specs=in_specs,
        out_specs=pl.BlockSpec((tm, tn), lambda i, j: (i, j)),
        out_shape=jax.ShapeDtypeStruct((m, n), out_dtype),
        scratch_shapes=scratch,
        compiler_params=_params("parallel", "arbitrary"),
        name=name,
    )(*args)


def _down_proj_kernel(x_ref, gkv_ref, gq_ref, wkv_ref, wq_ref, ckr_ref, cq_ref):
    chunk = x_ref.shape[0] // DOWN_PROJ_CHUNKS
    for r in range(DOWN_PROJ_CHUNKS):
        rows = slice(r * chunk, (r + 1) * chunk)
        x = x_ref[rows, :]
        xhat = x * lax.rsqrt(jnp.mean(x * x, axis=-1, keepdims=True) + NORM_EPS)
        ckr_ref[rows, :] = jnp.dot((xhat * gkv_ref[...]).astype(BF16), wkv_ref[...],
                                   preferred_element_type=F32)
        cq_ref[rows, :] = jnp.dot((xhat * gq_ref[...]).astype(BF16), wq_ref[...],
                                  preferred_element_type=F32)


def _down_proj(x, g_kv, g_q, w_dkv, w_dq, *, tm=1024):
    n_kv = w_dkv.shape[1]
    row = lambda i: (i, 0)
    full = lambda i: (0, 0)
    return pl.pallas_call(
        _down_proj_kernel,
        grid=(TOKENS // tm,),
        in_specs=[pl.BlockSpec((tm, D_MODEL), row),
                  pl.BlockSpec((1, D_MODEL), full),
                  pl.BlockSpec((1, D_MODEL), full),
                  pl.BlockSpec((D_MODEL, n_kv), full),
                  pl.BlockSpec((D_MODEL, Q_LORA), full)],
        out_specs=[pl.BlockSpec((tm, n_kv), row),
                   pl.BlockSpec((tm, Q_LORA), row)],
        out_shape=[jax.ShapeDtypeStruct((TOKENS, n_kv), F32),
                   jax.ShapeDtypeStruct((TOKENS, Q_LORA), F32)],
        compiler_params=_params("parallel"),
        name="mla_down_proj",
    )(x, g_kv.reshape(1, D_MODEL), g_q.reshape(1, D_MODEL), w_dkv, w_dq)


def _gelu_tanh(x):
    k = -2.0 * math.sqrt(2.0 / math.pi) * math.log2(math.e)
    e = jnp.exp2(x * (k + (k * 0.044715) * (x * x)))
    return x / (1.0 + e)


def _inproj_kernel(x_ref, g_ref, wa_ref, wb_ref, ba_ref, bb_ref, w1_ref, w2_ref,
                   gate_ref, rec_ref, w1_b_ref, w2_b_ref, xs_ref):
    w1_b_ref[...] = w1_ref[...].astype(w1_b_ref.dtype)
    w2_b_ref[...] = w2_ref[...].astype(w2_b_ref.dtype)

    def branch(xv):
        ga = jnp.dot(xv, wa_ref[...], preferred_element_type=F32) + ba_ref[...]
        gate_ref[...] = _gelu_tanh(ga).astype(gate_ref.dtype)
        rec_ref[...] = jnp.dot(xv, wb_ref[...], preferred_element_type=F32) + bb_ref[...]

    @pl.when(pl.program_id(1) == 0)
    def _():
        xv = _rms(x_ref[...], g_ref[...]).astype(BF16)
        xs_ref[...] = xv
        branch(xv)

    @pl.when(pl.program_id(1) != 0)
    def _():
        branch(xs_ref[...])


def _inproj(x, gain, w_in, b_in, sq1, sq2, *, tm=512, tn=1024):
    ni = TOKENS // tm
    nj = D_RNN // tn
    b2 = b_in.reshape(1, 2 * D_RNN)
    sq_blk = (D_MODEL // ni, D_MODEL // nj)
    sq_in = pl.BlockSpec((None,) + sq_blk, lambda i, j: (0, i, j))
    sq_out = pl.BlockSpec(sq_blk, lambda i, j: (i, j))
    sq_shape = jax.ShapeDtypeStruct((D_MODEL, D_MODEL), BF16)
    return pl.pallas_call(
        _inproj_kernel,
        grid=(ni, nj),
        in_specs=[pl.BlockSpec((tm, D_MODEL), lambda i, j: (i, 0)),
                  pl.BlockSpec((1, D_MODEL), lambda i, j: (0, 0)),
                  pl.BlockSpec((D_MODEL, tn), lambda i, j: (0, j)),
                  pl.BlockSpec((D_MODEL, tn), lambda i, j: (0, j + nj)),
                  pl.BlockSpec((1, tn), lambda i, j: (0, j)),
                  pl.BlockSpec((1, tn), lambda i, j: (0, j + nj)),
                  sq_in, sq_in],
        out_specs=[pl.BlockSpec((tm, tn), lambda i, j: (i, j)),
                   pl.BlockSpec((tm, tn), lambda i, j: (i, j)),
                   sq_out, sq_out],
        out_shape=[jax.ShapeDtypeStruct((TOKENS, D_RNN), BF16),
                   jax.ShapeDtypeStruct((TOKENS, D_RNN), F32),
                   sq_shape, sq_shape],
        scratch_shapes=[pltpu.VMEM((tm, D_MODEL), BF16)],
        compiler_params=_params("parallel", "arbitrary"),
        name="lru_in_proj",
    )(x, gain.reshape(1, D_MODEL), w_in, w_in, b2, b2, sq1, sq2)


def _rglru_kernel(x_ref, gate_ref, cw_ref, cb_ref, wg_ref, bg_ref, lam_ref, wi_ref, wo_ref,
                  y_ref, wi_b_ref, wo_b_ref,
                  xp_s, a_s, b_s, tail_s, eq_s, cm_s, hc_s, *, chunk):
    wi_b_ref[...] = wi_ref[...].astype(wi_b_ref.dtype)
    wo_b_ref[...] = wo_ref[...].astype(wo_b_ref.dtype)

    c = pl.program_id(1)
    seg = chunk // SUBLANES
    halo = SUBLANES * (CONV_W - 1)

    @pl.when(c == 0)
    def _():
        tail_s[...] = jnp.zeros_like(tail_s)
        hc_s[...] = jnp.zeros_like(hc_s)

    for g in range(LANE_GROUPS):
        lanes = slice(g * LANES, (g + 1) * LANES)
        for j in range(SUBLANES):
            xp_s[g, pl.ds(halo + j, seg, stride=SUBLANES), :] = x_ref[j * seg:(j + 1) * seg, lanes]

    first_sublane = lax.broadcasted_iota(jnp.int32, (SUBLANES, LANES), 0) == 0
    for g in range(LANE_GROUPS):
        for d in range(1, CONV_W):
            slot = slice(SUBLANES * (CONV_W - 1 - d), SUBLANES * (CONV_W - d))
            cur = xp_s[g, halo + SUBLANES * (seg - d):halo + SUBLANES * (seg - d + 1), :]
            prev = tail_s[g, slot, :]
            xp_s[g, slot, :] = jnp.where(first_sublane, pltpu.roll(prev, 1, 0),
                                         pltpu.roll(cur, 1, 0))
            tail_s[g, slot, :] = cur

    lam = lam_ref[...]
    sp = jnp.maximum(-lam, 0.0) + jnp.log1p(jnp.exp(-jnp.abs(lam)))
    neg_half_c_sp = (-0.5 * LRU_C) * sp
    per_blk = LRU_BLOCK // LANES
    for n in range(N_BLK):
        parts = []
        for sub in range(per_blk):
            g = n * per_blk + sub
            lanes = slice(g * LANES, (g + 1) * LANES)
            acc = cb_ref[:, lanes] + cw_ref[CONV_W - 1:CONV_W, lanes] * xp_s[g, halo:halo + chunk, :]
            for d in range(1, CONV_W):
                off = halo - SUBLANES * d
                acc = acc + cw_ref[CONV_W - 1 - d:CONV_W - d, lanes] * xp_s[g, off:off + chunk, :]
            parts.append(acc)
        xn = jnp.concatenate(parts, axis=1)
        cols = slice(n * LRU_BLOCK, (n + 1) * LRU_BLOCK)
        gh = jnp.dot(xn.astype(BF16), wg_ref[n], preferred_element_type=F32) + bg_ref[n:n + 1, :]
        tr = jnp.tanh(gh[:, :LRU_BLOCK])
        i = 0.5 * jnp.tanh(gh[:, LRU_BLOCK:]) + 0.5
        log_a = (tr + 1.0) * neg_half_c_sp[:, cols]
        a = jnp.exp(log_a)
        x1 = -jnp.tanh(log_a) * (a * a + 1.0)
        mult = jnp.where(x1 > 0.0, x1 * lax.rsqrt(x1), 0.0)
        b = mult * (i * xn)
        for sub in range(per_blk):
            lanes = slice(sub * LANES, (sub + 1) * LANES)
            a_s[n * per_blk + sub] = a[:, lanes]
            b_s[n * per_blk + sub] = b[:, lanes]

    def tile(ref, g, s):
        return ref[g, pl.ds(pl.multiple_of(s * SUBLANES, SUBLANES), SUBLANES), :]

    for g0 in range(0, LANE_GROUPS, SCAN_GROUPS):
        groups = range(g0, g0 + SCAN_GROUPS)

        def pass1(s, carry):
            hs, ps = carry
            a = [tile(a_s, g, s) for g in groups]
            b = [tile(b_s, g, s) for g in groups]
            return (tuple(ai * hi + bi for ai, hi, bi in zip(a, hs, b)),
                    tuple(ai * pi for ai, pi in zip(a, ps)))

        zero = jnp.zeros((SUBLANES, LANES), F32)
        one = jnp.ones((SUBLANES, LANES), F32)
        e, q = lax.fori_loop(0, seg, pass1, ((zero,) * SCAN_GROUPS, (one,) * SCAN_GROUPS),
                             unroll=8)
        for idx, g in enumerate(groups):
            eq_s[0] = e[idx]
            eq_s[1] = q[idx]
            h = hc_s[g, 0:1, :]
            for j in range(SUBLANES):
                cm_s[g, j:j + 1, :] = h
                h = eq_s[1, j:j + 1, :] * h + eq_s[0, j:j + 1, :]
            hc_s[g, 0:1, :] = h

        def pass2(s, hs):
            out = []
            for g, h in zip(groups, hs):
                h = tile(a_s, g, s) * h + tile(b_s, g, s)
                b_s[g, pl.ds(pl.multiple_of(s * SUBLANES, SUBLANES), SUBLANES), :] = h
                out.append(h)
            return tuple(out)

        lax.fori_loop(0, seg, pass2, tuple(cm_s[g] for g in groups), unroll=8)

    for g in range(LANE_GROUPS):
        lanes = slice(g * LANES, (g + 1) * LANES)
        for j in range(SUBLANES):
            rows = slice(j * seg, (j + 1) * seg)
            h = b_s[g, pl.ds(j, seg, stride=SUBLANES), :]
            y_ref[rows, lanes] = gate_ref[rows, lanes] * h.astype(y_ref.dtype)


def _rglru(rec_pre, gate, conv_w, conv_b, w_gate, b_gate, lam, ffn_w_in, ffn_w_out, layer, *,
           chunk=512):
    nc = SEQ // chunk
    steps = BATCH * nc
    wi_rows = D_MODEL // steps
    wo_rows = D_FF // steps
    row = lambda b, c: (b * nc + c, 0)
    band = lambda b, c: (layer, b * nc + c, 0)
    full2 = lambda b, c: (0, 0)
    return pl.pallas_call(
        functools.partial(_rglru_kernel, chunk=chunk),
        grid=(BATCH, nc),
        in_specs=[pl.BlockSpec((chunk, D_RNN), row),
                  pl.BlockSpec((chunk, D_RNN), row),
                  pl.BlockSpec((CONV_W, D_RNN), full2),
                  pl.BlockSpec((1, D_RNN), full2),
                  pl.BlockSpec((N_BLK, LRU_BLOCK, 2 * LRU_BLOCK), lambda b, c: (0, 0, 0)),
                  pl.BlockSpec((N_BLK, 2 * LRU_BLOCK), full2),
                  pl.BlockSpec((1, D_RNN), full2),
                  pl.BlockSpec((None, wi_rows, 2 * D_FF), band),
                  pl.BlockSpec((None, wo_rows, D_MODEL), band)],
        out_specs=[pl.BlockSpec((chunk, D_RNN), row),
                   pl.BlockSpec((wi_rows, 2 * D_FF), row),
                   pl.BlockSpec((wo_rows, D_MODEL), row)],
        out_shape=[jax.ShapeDtypeStruct((TOKENS, D_RNN), BF16),
                   jax.ShapeDtypeStruct((D_MODEL, 2 * D_FF), BF16),
                   jax.ShapeDtypeStruct((D_FF, D_MODEL), BF16)],
        scratch_shapes=[pltpu.VMEM((LANE_GROUPS, SUBLANES * (CONV_W - 1) + chunk, LANES), F32),
                        pltpu.VMEM((LANE_GROUPS, chunk, LANES), F32),
                        pltpu.VMEM((LANE_GROUPS, chunk, LANES), F32),
                        pltpu.VMEM((LANE_GROUPS, SUBLANES * (CONV_W - 1), LANES), F32),
                        pltpu.VMEM((2, SUBLANES, LANES), F32),
                        pltpu.VMEM((LANE_GROUPS, SUBLANES, LANES), F32),
                        pltpu.VMEM((LANE_GROUPS, SUBLANES, LANES), F32)],
        compiler_params=_params("parallel", "arbitrary"),
        name="rglru_core",
    )(rec_pre, gate, conv_w, conv_b.reshape(1, D_RNN), w_gate, b_gate, lam.reshape(1, D_RNN),
      ffn_w_in, ffn_w_out)


def _ffn_kernel(h_ref, g_ref, wa_ref, wb_ref, w2_ref, o_ref, hn_s):
    f = pl.program_id(1)

    def branch(hn):
        ua = jnp.dot(hn, wa_ref[...], preferred_element_type=F32)
        ub = jnp.dot(hn, wb_ref[...], preferred_element_type=F32)
        act = (jax.nn.silu(ua) * ub).astype(BF16)
        return jnp.dot(act, w2_ref[...], preferred_element_type=F32)

    @pl.when(f == 0)
    def _():
        h = h_ref[...]
        hn = _rms(h, g_ref[...]).astype(BF16)
        hn_s[...] = hn
        o_ref[...] = h + branch(hn)

    @pl.when(f != 0)
    def _():
        o_ref[...] += branch(hn_s[...])


def _ffn(h, gain, w_in, w_out, *, tm=1024, tf=512):
    nf = D_FF // tf
    return pl.pallas_call(
        _ffn_kernel,
        grid=(TOKENS // tm, nf),
        in_specs=[pl.BlockSpec((tm, D_MODEL), lambda i, f: (i, 0)),
                  pl.BlockSpec((1, D_MODEL), lambda i, f: (0, 0)),
                  pl.BlockSpec((D_MODEL, tf), lambda i, f: (0, f)),
                  pl.BlockSpec((D_MODEL, tf), lambda i, f: (0, f + nf)),
                  pl.BlockSpec((tf, D_MODEL), lambda i, f: (f, 0))],
        out_specs=pl.BlockSpec((tm, D_MODEL), lambda i, f: (i, 0)),
        out_shape=jax.ShapeDtypeStruct((TOKENS, D_MODEL), F32),
        scratch_shapes=[pltpu.VMEM((tm, D_MODEL), BF16)],
        compiler_params=_params("parallel", "arbitrary"),
        name="swiglu_ffn",
    )(h, gain.reshape(1, D_MODEL), w_in, w_in, w_out)


def _kv_heads_kernel(c_ref, kr_ref, gl_ref, w_ref, gn_ref, gr_ref, rc_ref, rs_ref,
                     k_ref, v_ref, cn_s, krr_s, ssr_s, *, heads):
    @pl.when(pl.program_id(1) == 0)
    def _():
        cn_s[...] = _rms(c_ref[...], gl_ref[...]).astype(BF16)
        kr = kr_ref[...]
        ssr_s[...] = jnp.broadcast_to(jnp.sum(kr * kr, axis=1, keepdims=True), ssr_s.shape)
        krr_s[...] = _rope(kr * gr_ref[...], rc_ref[...], rs_ref[...])

    ones = jnp.ones((v_ref.shape[0], HEAD_PAD - V_D), v_ref.dtype)
    kv_all = jnp.dot(cn_s[...], w_ref[...], preferred_element_type=F32)
    for g in range(heads):
        base = g * HEAD_PAD
        kv = kv_all[:, base:base + HEAD_PAD]
        kn = kv[:, :NOPE_D]
        ss = jnp.sum(kn * kn, axis=1, keepdims=True) + ssr_s[...]
        rstd = lax.rsqrt(ss * (1.0 / QK_D) + NORM_EPS)
        k_ref[:, base:base + NOPE_D] = (kn * rstd * gn_ref[...]).astype(k_ref.dtype)
        k_ref[:, base + NOPE_D:base + HEAD_PAD] = (krr_s[...] * rstd).astype(k_ref.dtype)
        v_ref[:, base:base + V_D] = kv[:, NOPE_D:].astype(v_ref.dtype)
        v_ref[:, base + V_D:base + HEAD_PAD] = ones


def _kv_heads(ckr, gl, w_ukv, gn, gr, rc, rs, *, tm=1024, heads=8):
    rope_blk = KV_LORA // LANES
    row = lambda i, h: (i, 0)
    vec = lambda i, h: (0, 0)
    return pl.pallas_call(
        functools.partial(_kv_heads_kernel, heads=heads),
        grid=(TOKENS // tm, N_HEADS // heads),
        in_specs=[pl.BlockSpec((tm, KV_LORA), row),
                  pl.BlockSpec((tm, LANES), lambda i, h: (i, rope_blk)),
                  pl.BlockSpec((1, KV_LORA), vec),
                  pl.BlockSpec((KV_LORA, heads * HEAD_PAD), lambda i, h: (0, h)),
                  pl.BlockSpec((1, LANES), vec),
                  pl.BlockSpec((1, LANES), vec),
                  pl.BlockSpec((tm, LANES), row),
                  pl.BlockSpec((tm, LANES), row)],
        out_specs=[pl.BlockSpec((tm, heads * HEAD_PAD), lambda i, h: (i, h)),
                   pl.BlockSpec((tm, heads * HEAD_PAD), lambda i, h: (i, h))],
        out_shape=[jax.ShapeDtypeStruct((TOKENS, N_HEADS * HEAD_PAD), BF16),
                   jax.ShapeDtypeStruct((TOKENS, N_HEADS * HEAD_PAD), BF16)],
        scratch_shapes=[pltpu.VMEM((tm, KV_LORA), BF16),
                        pltpu.VMEM((tm, LANES), F32),
                        pltpu.VMEM((tm, LANES), F32)],
        compiler_params=_params("parallel", "arbitrary"),
        name="kv_heads",
    )(ckr, ckr, gl.reshape(1, KV_LORA), w_ukv, gn, gr, rc, rs)


def _q_heads_kernel(c_ref, gl_ref, w_ref, gn_ref, gr_ref, gs_ref, rc_ref, rs_ref,
                    q_ref, cn_s, cg_s, sg_s, *, heads):
    @pl.when(pl.program_id(1) == 0)
    def _():
        cn_s[...] = _rms(c_ref[...], gl_ref[...]).astype(BF16)
        cg_s[...] = rc_ref[...] * gr_ref[...]
        sg_s[...] = rs_ref[...] * gs_ref[...]

    q_all = jnp.dot(cn_s[...], w_ref[...], preferred_element_type=F32)
    for g in range(heads):
        q = q_all[:, g * Q_HEAD_COLS:(g + 1) * Q_HEAD_COLS]
        qn = q[:, :NOPE_D]
        qr = q[:, NOPE_D:NOPE_D + LANES]
        qs = q[:, NOPE_D + LANES:]
        ss = jnp.sum(qn * qn + qr * qr, axis=1, keepdims=True)
        rstd = lax.rsqrt(ss * (1.0 / QK_D) + NORM_EPS)
        base = g * HEAD_PAD
        q_ref[:, base:base + NOPE_D] = (qn * rstd * gn_ref[...]).astype(q_ref.dtype)
        roped = qr * cg_s[...] + qs * sg_s[...]
        q_ref[:, base + NOPE_D:base + HEAD_PAD] = (roped * rstd).astype(q_ref.dtype)


def _q_heads(cq, gl, w_uq3, gn, gr, gs, rc, rs, *, tm=1024, heads=8):
    row = lambda i, h: (i, 0)
    vec = lambda i, h: (0, 0)
    return pl.pallas_call(
        functools.partial(_q_heads_kernel, heads=heads),
        grid=(TOKENS // tm, N_HEADS // heads),
        in_specs=[pl.BlockSpec((tm, Q_LORA), row),
                  pl.BlockSpec((1, Q_LORA), vec),
                  pl.BlockSpec((Q_LORA, heads * Q_HEAD_COLS), lambda i, h: (0, h)),
                  pl.BlockSpec((1, LANES), vec),
                  pl.BlockSpec((1, LANES), vec),
                  pl.BlockSpec((1, LANES), vec),
                  pl.BlockSpec((tm, LANES), row),
                  pl.BlockSpec((tm, LANES), row)],
        out_specs=pl.BlockSpec((tm, heads * HEAD_PAD), lambda i, h: (i, h)),
        out_shape=jax.ShapeDtypeStruct((TOKENS, N_HEADS * HEAD_PAD), BF16),
        scratch_shapes=[pltpu.VMEM((tm, Q_LORA), BF16),
                        pltpu.VMEM((tm, LANES), F32),
                        pltpu.VMEM((tm, LANES), F32)],
        compiler_params=_params("parallel", "arbitrary"),
        name="q_heads",
    )(cq, gl.reshape(1, Q_LORA), w_uq3, gn, gr, gs, rc, rs)


def _attn_kernel(q_ref, k_ref, v_ref, wi_ref, wo_ref, o_ref, wi_b_ref, wo_b_ref, m_s, acc_s, *,
                 blk, heads, subs):
    wi_b_ref[...] = wi_ref[...].astype(wi_b_ref.dtype)
    wo_b_ref[...] = wo_ref[...].astype(wo_b_ref.dtype)
    i = pl.program_id(2)
    m_s[...] = jnp.full_like(m_s, NEG_BIG)
    acc_s[...] = jnp.zeros_like(acc_s)

    def chain(g, row0, nrows, start, nkeys, mask_off):
        cols = slice(g * HEAD_PAD, (g + 1) * HEAD_PAD)
        rows = slice(row0, row0 + nrows)
        k = k_ref[pl.ds(start, nkeys), cols]
        v = v_ref[pl.ds(start, nkeys), cols]
        s = lax.dot_general(q_ref[rows, cols], k, (((1,), (1,)), ((), ())),
                            preferred_element_type=F32)
        if mask_off is not None:
            row = lax.broadcasted_iota(jnp.int32, (nrows, nkeys), 0)
            col = lax.broadcasted_iota(jnp.int32, (nrows, nkeys), 1)
            s = jnp.where(col <= row + mask_off, s, NEG_BIG)
        m_prev = m_s[g, rows, :]
        m_new = jnp.maximum(m_prev, jnp.max(s, axis=1, keepdims=True))
        p = jnp.exp2(s - jnp.concatenate([m_new] * (nkeys // LANES), axis=1))
        alpha = jnp.exp2(m_prev - m_new)
        pv = jnp.dot(p.astype(BF16), v, preferred_element_type=F32)
        acc_s[g, rows, :] = (jnp.concatenate([alpha] * (HEAD_PAD // LANES), axis=1)
                             * acc_s[g, rows, :] + pv)
        m_s[g, rows, :] = m_new

    def step(j, active):
        start = pl.multiple_of(j * blk, blk)
        for g in range(heads):
            for u, masked in active:
                chain(g, u * blk, blk, start, blk, 0 if masked else None)

    def body(j, carry):
        for r in range(KV_UNROLL):
            step(KV_UNROLL * j + r, [(u, False) for u in range(subs)])
        return carry

    assert subs % KV_UNROLL == 0
    lax.fori_loop(0, (subs // KV_UNROLL) * i, body, 0)
    for d in range(subs):
        step(subs * i + d, [(d, True)] + [(u, False) for u in range(d + 1, subs)])
    for g in range(heads):
        acc = acc_s[g]
        o_ref[:, g * V_D:(g + 1) * V_D] = (acc[:, :V_D] / acc[:, V_D:]).astype(o_ref.dtype)


def _attention(q, k, v, ffn_w_in, ffn_w_out, layer, *, blk=512, heads=2, subs=4):
    tq = subs * blk
    nq = SEQ // tq
    nh = N_HEADS // heads
    bands = BATCH * nh
    wi_blk = (D_MODEL // bands, 2 * D_FF // nq)
    wo_blk = (D_FF // bands, D_MODEL // nq)
    tile = lambda b, h, i: (b * nh + h, i)
    src_tile = lambda b, h, i: (layer, b * nh + h, i)
    return pl.pallas_call(
        functools.partial(_attn_kernel, blk=blk, heads=heads, subs=subs),
        grid=(BATCH, nh, nq),
        in_specs=[pl.BlockSpec((tq, heads * HEAD_PAD), lambda b, h, i: (b * nq + i, h)),
                  pl.BlockSpec((SEQ, heads * HEAD_PAD), lambda b, h, i: (b, h)),
                  pl.BlockSpec((SEQ, heads * HEAD_PAD), lambda b, h, i: (b, h)),
                  pl.BlockSpec((None,) + wi_blk, src_tile),
                  pl.BlockSpec((None,) + wo_blk, src_tile)],
        out_specs=[pl.BlockSpec((tq, heads * V_D), lambda b, h, i: (b * nq + i, h)),
                   pl.BlockSpec(wi_blk, tile),
                   pl.BlockSpec(wo_blk, tile)],
        out_shape=[jax.ShapeDtypeStruct((TOKENS, N_HEADS * V_D), BF16),
                   jax.ShapeDtypeStruct((D_MODEL, 2 * D_FF), BF16),
                   jax.ShapeDtypeStruct((D_FF, D_MODEL), BF16)],
        scratch_shapes=[pltpu.VMEM((heads, tq, LANES), F32),
                        pltpu.VMEM((heads, tq, HEAD_PAD), F32)],
        compiler_params=_params("parallel", "parallel", "arbitrary"),
        name="flash_attention",
    )(q, k, v, ffn_w_in, ffn_w_out)


def _swap_halves(t):
    half = ROPE_D // 2
    return jnp.concatenate([t[..., half:], t[..., :half]], axis=-1)


def _split_qk_gain(g, scale):
    rope = lambda v: _spread_rope(v * scale).reshape(1, LANES)
    return (g[:NOPE_D] * scale).reshape(1, LANES), rope(g[NOPE_D:]), rope(_swap_halves(g[NOPE_D:]))


def kernel(x, positions, norm_mix, norm_ffn, lru_w_in, lru_b_in, lru_conv_w, lru_conv_b, lru_w_gate, lru_b_gate, lru_lambda, lru_w_out, lru_b_out, kv_norm_in, w_dkv, kv_latent_norm, w_ukv, k_norm, w_dq, q_latent_norm, w_uq, q_norm, w_o, ffn_w_in, ffn_w_out):
    h = x.reshape(TOKENS, D_MODEL)
    rc, rs = _rope_tables(positions)

    gate, rec_pre, lru_w_out_b, w_o_b = _inproj(h, norm_mix[0], lru_w_in[0].astype(BF16),
                                                lru_b_in[0], lru_w_out, w_o)
    y, ffn_wi0, ffn_wo0 = _rglru(rec_pre, gate, lru_conv_w[0], lru_conv_b[0],
                                 (0.5 * lru_w_gate[0]).astype(BF16), 0.5 * lru_b_gate[0],
                                 lru_lambda[0], ffn_w_in, ffn_w_out, 0)
    h = _linear(y, lru_w_out_b, bias=lru_b_out[0], resid=h, tn=D_MODEL,
                name="lru_out_proj")
    h = _ffn(h, norm_ffn[0], ffn_wi0, ffn_wo0)

    w_dkv_p = jnp.concatenate([w_dkv[:, :KV_LORA], _spread_rope(w_dkv[:, KV_LORA:])],
                              axis=1).astype(BF16)
    ckr, cq = _down_proj(h, kv_norm_in, norm_mix[1], w_dkv_p, w_dq[0].astype(BF16))
    kgn, kgr, _ = _split_qk_gain(k_norm, 1.0)
    k_sh, v_sh = _kv_heads(ckr, kv_latent_norm,
                           w_ukv.reshape(KV_LORA, N_HEADS * (NOPE_D + V_D)).astype(BF16),
                           kgn, kgr, rc, rs)

    w_rope = w_uq[0][..., NOPE_D:]
    w_uq3 = jnp.concatenate([w_uq[0][..., :NOPE_D], _spread_rope(w_rope),
                             _spread_rope(_swap_halves(w_rope))], axis=-1)
    qgn, qgr, qgs = _split_qk_gain(q_norm[0], math.log2(math.e) / math.sqrt(QK_D))
    q = _q_heads(cq, q_latent_norm[0],
                 w_uq3.reshape(Q_LORA, N_HEADS * Q_HEAD_COLS).astype(BF16),
                 qgn, qgr, qgs, rc, rs)
    o, ffn_wi1, ffn_wo1 = _attention(q, k_sh, v_sh, ffn_w_in, ffn_w_out, 1)
    h = _linear(o, w_o_b, resid=h, tn=D_MODEL, name="attn_out_proj")
    h = _ffn(h, norm_ffn[1], ffn_wi1, ffn_wo1)
    return h.reshape(BATCH, SEQ, D_MODEL)
```

```python
import functools
import math

import jax
import jax.numpy as jnp
from jax import lax
from jax.experimental import pallas as pl
from jax.experimental.pallas import tpu as pltpu

F32 = jnp.float32
BF16 = jnp.bfloat16

D_MODEL = 2048
BATCH = 4
SEQ = 4096
TOKENS = BATCH * SEQ
D_RNN = D_MODEL
LRU_BLOCK = 256
N_BLK = D_RNN // LRU_BLOCK
CONV_W = 4
LRU_C = 8.0
N_HEADS = 16
NOPE_D = 128
ROPE_D = 64
QK_D = NOPE_D + ROPE_D
V_D = 128
Q_LORA = 512
KV_LORA = 512
ROPE_THETA = 10000.0
D_FF = 5632
NORM_EPS = 1e-6

LANES = 128
SUBLANES = 8
HEAD_PAD = 2 * LANES
ROPE_T2_LANE = LANES // 2
KV_UNROLL = 4
Q_HEAD_COLS = 3 * LANES
DOWN_PROJ_CHUNKS = 4
LANE_GROUPS = D_RNN // LANES
SCAN_GROUPS = 8
V7X_VMEM_BYTES = 64 * 1024 * 1024
VMEM_LIMIT = V7X_VMEM_BYTES * 7 // 8
NEG_BIG = -1e30


def _params(*semantics):
    return pltpu.CompilerParams(dimension_semantics=semantics, vmem_limit_bytes=VMEM_LIMIT)


def _rms(x, g):
    ms = jnp.mean(x * x, axis=-1, keepdims=True)
    return x * lax.rsqrt(ms + NORM_EPS) * g


def _rope(t, c, s):
    return t * c + pltpu.roll(t, ROPE_T2_LANE, 1) * s


def _spread_rope(v):
    half = ROPE_D // 2
    z = jnp.zeros(v.shape[:-1] + (ROPE_T2_LANE - half,), v.dtype)
    return jnp.concatenate([v[..., :half], z, v[..., half:], z], axis=-1)


def _rope_table_kernel(pos_ref, invf_ref, rc_ref, rs_ref):
    half = ROPE_D // 2
    per_row = LANES // half
    tr = pos_ref.shape[0]
    ang = pos_ref[...].astype(F32) * invf_ref[...]
    cos = jnp.cos(ang)
    sin = jnp.sin(ang)
    low = lax.broadcasted_iota(jnp.int32, (tr, LANES), 1) < half
    for p in range(per_row):
        shift = (LANES - half * p) % LANES
        c = jnp.where(low, pltpu.roll(cos, shift, 1) if shift else cos, 0.0)
        s = jnp.where(low, pltpu.roll(sin, shift, 1) if shift else sin, 0.0)
        rc_ref[pl.ds(p, tr, stride=per_row), :] = c + pltpu.roll(c, ROPE_T2_LANE, 1)
        rs_ref[pl.ds(p, tr, stride=per_row), :] = pltpu.roll(s, ROPE_T2_LANE, 1) - s


def _rope_tables(positions):
    half = ROPE_D // 2
    per_row = LANES // half
    rows = TOKENS // per_row
    inv_freq = ROPE_THETA ** (-jnp.arange(0, ROPE_D, 2, dtype=F32) / ROPE_D)
    pos = jnp.broadcast_to(positions.reshape(rows, per_row, 1), (rows, per_row, half))
    pos = pos.reshape(rows, LANES)
    invf = jnp.tile(inv_freq, per_row).reshape(1, LANES)
    tr = 512
    return pl.pallas_call(
        _rope_table_kernel,
        grid=(rows // tr,),
        in_specs=[pl.BlockSpec((tr, LANES), lambda i: (i, 0)),
                  pl.BlockSpec((1, LANES), lambda i: (0, 0))],
        out_specs=[pl.BlockSpec((per_row * tr, LANES), lambda i: (i, 0)),
                   pl.BlockSpec((per_row * tr, LANES), lambda i: (i, 0))],
        out_shape=[jax.ShapeDtypeStruct((TOKENS, LANES), F32)] * 2,
        compiler_params=_params("arbitrary"),
        name="rope_tables",
    )(pos, invf)


def _linear_kernel(*refs, has_gain, has_bias, has_resid, stage_x):
    refs = list(refs)
    x_ref = refs.pop(0)
    w_ref = refs.pop(0)
    g_ref = refs.pop(0) if has_gain else None
    b_ref = refs.pop(0) if has_bias else None
    r_ref = refs.pop(0) if has_resid else None
    o_ref = refs.pop(0)
    if stage_x:
        xs_ref = refs.pop(0)

        @pl.when(pl.program_id(1) == 0)
        def _():
            x = x_ref[...].astype(F32)
            if has_gain:
                x = _rms(x, g_ref[...])
            xs_ref[...] = x.astype(BF16)

        xv = xs_ref[...]
    else:
        xv = x_ref[...]
    acc = jnp.dot(xv, w_ref[...], preferred_element_type=F32)
    if has_bias:
        acc = acc + b_ref[...]
    if has_resid:
        acc = acc + r_ref[...]
    o_ref[...] = acc.astype(o_ref.dtype)


def _linear(x, w, *, gain=None, bias=None, resid=None, out_dtype=F32, tm=512, tn=1024, name):
    m, k = x.shape
    n = w.shape[1]
    tn = min(tn, n)
    stage_x = gain is not None or x.dtype != BF16
    in_specs = [pl.BlockSpec((tm, k), lambda i, j: (i, 0)),
                pl.BlockSpec((k, tn), lambda i, j: (0, j))]
    args = [x, w]
    if gain is not None:
        in_specs.append(pl.BlockSpec((1, k), lambda i, j: (0, 0)))
        args.append(gain.reshape(1, k))
    if bias is not None:
        in_specs.append(pl.BlockSpec((1, tn), lambda i, j: (0, j)))
        args.append(bias.reshape(1, n))
    if resid is not None:
        in_specs.append(pl.BlockSpec((tm, tn), lambda i, j: (i, j)))
        args.append(resid)
    scratch = [pltpu.VMEM((tm, k), BF16)] if stage_x else []
    return pl.pallas_call(
        functools.partial(_linear_kernel, has_gain=gain is not None, has_bias=bias is not None,
                          has_resid=resid is not None, stage_x=stage_x),
        grid=(m // tm, n // tn),
        in_specs=in_specs,
        out_specs=pl.BlockSpec((tm, tn), lambda i, j: (i, j)),
        out_shape=jax.ShapeDtypeStruct((m, n), out_dtype),
        scratch_shapes=scratch,
        compiler_params=_params("parallel", "arbitrary"),
        name=name,
    )(*args)


def _down_proj_kernel(x_ref, gkv_ref, gq_ref, wkv_ref, wq_ref, ckr_ref, cq_ref):
    chunk = x_ref.shape[0] // DOWN_PROJ_CHUNKS
    for r in range(DOWN_PROJ_CHUNKS):
        rows = slice(r * chunk, (r + 1) * chunk)
        x = x_ref[rows, :]
        xhat = x * lax.rsqrt(jnp.mean(x * x, axis=-1, keepdims=True) + NORM_EPS)
        ckr_ref[rows, :] = jnp.dot((xhat * gkv_ref[...]).astype(BF16), wkv_ref[...],
                                   preferred_element_type=F32)
        cq_ref[rows, :] = jnp.dot((xhat * gq_ref[...]).astype(BF16), wq_ref[...],
                                  preferred_element_type=F32)


def _down_proj(x, g_kv, g_q, w_dkv, w_dq, *, tm=1024):
    n_kv = w_dkv.shape[1]
    row = lambda i: (i, 0)
    full = lambda i: (0, 0)
    return pl.pallas_call(
        _down_proj_kernel,
        grid=(TOKENS // tm,),
        in_specs=[pl.BlockSpec((tm, D_MODEL), row),
                  pl.BlockSpec((1, D_MODEL), full),
                  pl.BlockSpec((1, D_MODEL), full),
                  pl.BlockSpec((D_MODEL, n_kv), full),
                  pl.BlockSpec((D_MODEL, Q_LORA), full)],
        out_specs=[pl.BlockSpec((tm, n_kv), row),
                   pl.BlockSpec((tm, Q_LORA), row)],
        out_shape=[jax.ShapeDtypeStruct((TOKENS, n_kv), F32),
                   jax.ShapeDtypeStruct((TOKENS, Q_LORA), F32)],
        compiler_params=_params("parallel"),
        name="mla_down_proj",
    )(x, g_kv.reshape(1, D_MODEL), g_q.reshape(1, D_MODEL), w_dkv, w_dq)


def _gelu_tanh(x):
    k = -2.0 * math.sqrt(2.0 / math.pi) * math.log2(math.e)
    e = jnp.exp2(x * (k + (k * 0.044715) * (x * x)))
    return x / (1.0 + e)


def _inproj_kernel(x_ref, g_ref, wa_ref, wb_ref, ba_ref, bb_ref, w1_ref, w2_ref,
                   gate_ref, rec_ref, w1_b_ref, w2_b_ref, xs_ref):
    w1_b_ref[...] = w1_ref[...].astype(w1_b_ref.dtype)
    w2_b_ref[...] = w2_ref[...].astype(w2_b_ref.dtype)

    def branch(xv):
        ga = jnp.dot(xv, wa_ref[...], preferred_element_type=F32) + ba_ref[...]
        gate_ref[...] = _gelu_tanh(ga).astype(gate_ref.dtype)
        rec_ref[...] = jnp.dot(xv, wb_ref[...], preferred_element_type=F32) + bb_ref[...]

    @pl.when(pl.program_id(1) == 0)
    def _():
        xv = _rms(x_ref[...], g_ref[...]).astype(BF16)
        xs_ref[...] = xv
        branch(xv)

    @pl.when(pl.program_id(1) != 0)
    def _():
        branch(xs_ref[...])


def _inproj(x, gain, w_in, b_in, sq1, sq2, *, tm=1024, tn=512):
    ni = TOKENS // tm
    nj = D_RNN // tn
    b2 = b_in.reshape(1, 2 * D_RNN)
    sq_blk = (D_MODEL // ni, D_MODEL // nj)
    sq_in = pl.BlockSpec((None,) + sq_blk, lambda i, j: (0, i, j))
    sq_out = pl.BlockSpec(sq_blk, lambda i, j: (i, j))
    sq_shape = jax.ShapeDtypeStruct((D_MODEL, D_MODEL), BF16)
    return pl.pallas_call(
        _inproj_kernel,
        grid=(ni, nj),
        in_specs=[pl.BlockSpec((tm, D_MODEL), lambda i, j: (i, 0)),
                  pl.BlockSpec((1, D_MODEL), lambda i, j: (0, 0)),
                  pl.BlockSpec((D_MODEL, tn), lambda i, j: (0, j)),
                  pl.BlockSpec((D_MODEL, tn), lambda i, j: (0, j + nj)),
                  pl.BlockSpec((1, tn), lambda i, j: (0, j)),
                  pl.BlockSpec((1, tn), lambda i, j: (0, j + nj)),
                  sq_in, sq_in],
        out_specs=[pl.BlockSpec((tm, tn), lambda i, j: (i, j)),
                   pl.BlockSpec((tm, tn), lambda i, j: (i, j)),
                   sq_out, sq_out],
        out_shape=[jax.ShapeDtypeStruct((TOKENS, D_RNN), BF16),
                   jax.ShapeDtypeStruct((TOKENS, D_RNN), F32),
                   sq_shape, sq_shape],
        scratch_shapes=[pltpu.VMEM((tm, D_MODEL), BF16)],
        compiler_params=_params("parallel", "arbitrary"),
        name="lru_in_proj",
    )(x, gain.reshape(1, D_MODEL), w_in, w_in, b2, b2, sq1, sq2)


def _rglru_kernel(x_ref, gate_ref, cw_ref, cb_ref, wg_ref, bg_ref, lam_ref, wi_ref, wo_ref,
                  y_ref, wi_b_ref, wo_b_ref,
                  xp_s, a_s, b_s, tail_s, eq_s, cm_s, hc_s, *, chunk):
    wi_b_ref[...] = wi_ref[...].astype(wi_b_ref.dtype)
    wo_b_ref[...] = wo_ref[...].astype(wo_b_ref.dtype)

    c = pl.program_id(1)
    seg = chunk // SUBLANES
    halo = SUBLANES * (CONV_W - 1)

    @pl.when(c == 0)
    def _():
        tail_s[...] = jnp.zeros_like(tail_s)
        hc_s[...] = jnp.zeros_like(hc_s)

    for g in range(LANE_GROUPS):
        lanes = slice(g * LANES, (g + 1) * LANES)
        for j in range(SUBLANES):
            xp_s[g, pl.ds(halo + j, seg, stride=SUBLANES), :] = x_ref[j * seg:(j + 1) * seg, lanes]

    first_sublane = lax.broadcasted_iota(jnp.int32, (SUBLANES, LANES), 0) == 0
    for g in range(LANE_GROUPS):
        for d in range(1, CONV_W):
            slot = slice(SUBLANES * (CONV_W - 1 - d), SUBLANES * (CONV_W - d))
            cur = xp_s[g, halo + SUBLANES * (seg - d):halo + SUBLANES * (seg - d + 1), :]
            prev = tail_s[g, slot, :]
            xp_s[g, slot, :] = jnp.where(first_sublane, pltpu.roll(prev, 1, 0),
                                         pltpu.roll(cur, 1, 0))
            tail_s[g, slot, :] = cur

    lam = lam_ref[...]
    sp = jnp.maximum(-lam, 0.0) + jnp.log1p(jnp.exp(-jnp.abs(lam)))
    half_c_sp = (0.5 * LRU_C) * sp
    exp2_scale = (-math.log2(math.e)) * half_c_sp
    per_blk = LRU_BLOCK // LANES
    for n in range(N_BLK):
        parts = []
        for sub in range(per_blk):
            g = n * per_blk + sub
            lanes = slice(g * LANES, (g + 1) * LANES)
            acc = cb_ref[:, lanes] + cw_ref[CONV_W - 1:CONV_W, lanes] * xp_s[g, halo:halo + chunk, :]
            for d in range(1, CONV_W):
                off = halo - SUBLANES * d
                acc = acc + cw_ref[CONV_W - 1 - d:CONV_W - d, lanes] * xp_s[g, off:off + chunk, :]
            parts.append(acc)
        xn = jnp.concatenate(parts, axis=1)
        cols = slice(n * LRU_BLOCK, (n + 1) * LRU_BLOCK)
        gh = jnp.dot(xn.astype(BF16), wg_ref[n], preferred_element_type=F32) + bg_ref[n:n + 1, :]
        tr = jnp.tanh(gh[:, :LRU_BLOCK])
        i = 0.5 * jnp.tanh(gh[:, LRU_BLOCK:]) + 0.5
        u = tr + 1.0
        neg_log_a = u * half_c_sp[:, cols]
        a = jnp.exp2(u * exp2_scale[:, cols])
        x1 = jnp.tanh(neg_log_a) * (a * a + 1.0)
        mult = jnp.where(x1 > 0.0, x1 * lax.rsqrt(x1), 0.0)
        b = mult * (i * xn)
        for sub in range(per_blk):
            lanes = slice(sub * LANES, (sub + 1) * LANES)
            a_s[n * per_blk + sub] = a[:, lanes]
            b_s[n * per_blk + sub] = b[:, lanes]

    def tile(ref, g, s):
        return ref[g, pl.ds(pl.multiple_of(s * SUBLANES, SUBLANES), SUBLANES), :]

    for g0 in range(0, LANE_GROUPS, SCAN_GROUPS):
        groups = range(g0, g0 + SCAN_GROUPS)

        def pass1(s, carry):
            hs, ps = carry
            a = [tile(a_s, g, s) for g in groups]
            b = [tile(b_s, g, s) for g in groups]
            return (tuple(ai * hi + bi for ai, hi, bi in zip(a, hs, b)),
                    tuple(ai * pi for ai, pi in zip(a, ps)))

        zero = jnp.zeros((SUBLANES, LANES), F32)
        one = jnp.ones((SUBLANES, LANES), F32)
        e, q = lax.fori_loop(0, seg, pass1, ((zero,) * SCAN_GROUPS, (one,) * SCAN_GROUPS),
                             unroll=8)
        for idx, g in enumerate(groups):
            eq_s[0] = e[idx]
            eq_s[1] = q[idx]
            h = hc_s[g, 0:1, :]
            for j in range(SUBLANES):
                cm_s[g, j:j + 1, :] = h
                h = eq_s[1, j:j + 1, :] * h + eq_s[0, j:j + 1, :]
            hc_s[g, 0:1, :] = h

        def pass2(s, hs):
            out = []
            for g, h in zip(groups, hs):
                h = tile(a_s, g, s) * h + tile(b_s, g, s)
                b_s[g, pl.ds(pl.multiple_of(s * SUBLANES, SUBLANES), SUBLANES), :] = h
                out.append(h)
            return tuple(out)

        lax.fori_loop(0, seg, pass2, tuple(cm_s[g] for g in groups), unroll=8)

    for g in range(LANE_GROUPS):
        lanes = slice(g * LANES, (g + 1) * LANES)
        for j in range(SUBLANES):
            rows = slice(j * seg, (j + 1) * seg)
            h = b_s[g, pl.ds(j, seg, stride=SUBLANES), :]
            y_ref[rows, lanes] = gate_ref[rows, lanes] * h.astype(y_ref.dtype)


def _rglru(rec_pre, gate, conv_w, conv_b, w_gate, b_gate, lam, ffn_w_in, ffn_w_out, layer, *,
           chunk=512):
    nc = SEQ // chunk
    steps = BATCH * nc
    wi_rows = D_MODEL // steps
    wo_rows = D_FF // steps
    row = lambda b, c: (b * nc + c, 0)
    band = lambda b, c: (layer, b * nc + c, 0)
    full2 = lambda b, c: (0, 0)
    return pl.pallas_call(
        functools.partial(_rglru_kernel, chunk=chunk),
        grid=(BATCH, nc),
        in_specs=[pl.BlockSpec((chunk, D_RNN), row),
                  pl.BlockSpec((chunk, D_RNN), row),
                  pl.BlockSpec((CONV_W, D_RNN), full2),
                  pl.BlockSpec((1, D_RNN), full2),
                  pl.BlockSpec((N_BLK, LRU_BLOCK, 2 * LRU_BLOCK), lambda b, c: (0, 0, 0)),
                  pl.BlockSpec((N_BLK, 2 * LRU_BLOCK), full2),
                  pl.BlockSpec((1, D_RNN), full2),
                  pl.BlockSpec((None, wi_rows, 2 * D_FF), band),
                  pl.BlockSpec((None, wo_rows, D_MODEL), band)],
        out_specs=[pl.BlockSpec((chunk, D_RNN), row),
                   pl.BlockSpec((wi_rows, 2 * D_FF), row),
                   pl.BlockSpec((wo_rows, D_MODEL), row)],
        out_shape=[jax.ShapeDtypeStruct((TOKENS, D_RNN), BF16),
                   jax.ShapeDtypeStruct((D_MODEL, 2 * D_FF), BF16),
                   jax.ShapeDtypeStruct((D_FF, D_MODEL), BF16)],
        scratch_shapes=[pltpu.VMEM((LANE_GROUPS, SUBLANES * (CONV_W - 1) + chunk, LANES), F32),
                        pltpu.VMEM((LANE_GROUPS, chunk, LANES), F32),
                        pltpu.VMEM((LANE_GROUPS, chunk, LANES), F32),
                        pltpu.VMEM((LANE_GROUPS, SUBLANES * (CONV_W - 1), LANES), F32),
                        pltpu.VMEM((2, SUBLANES, LANES), F32),
                        pltpu.VMEM((LANE_GROUPS, SUBLANES, LANES), F32),
                        pltpu.VMEM((LANE_GROUPS, SUBLANES, LANES), F32)],
        compiler_params=_params("parallel", "arbitrary"),
        name="rglru_core",
    )(rec_pre, gate, conv_w, conv_b.reshape(1, D_RNN), w_gate, b_gate, lam.reshape(1, D_RNN),
      ffn_w_in, ffn_w_out)


def _ffn_kernel(h_ref, g_ref, wa_ref, wb_ref, w2_ref, o_ref, hn_s):
    f = pl.program_id(1)

    def branch(hn):
        ua = jnp.dot(hn, wa_ref[...], preferred_element_type=F32)
        ub = jnp.dot(hn, wb_ref[...], preferred_element_type=F32)
        act = (jax.nn.silu(ua) * ub).astype(BF16)
        return jnp.dot(act, w2_ref[...], preferred_element_type=F32)

    @pl.when(f == 0)
    def _():
        h = h_ref[...]
        hn = _rms(h, g_ref[...]).astype(BF16)
        hn_s[...] = hn
        o_ref[...] = h + branch(hn)

    @pl.when(f != 0)
    def _():
        o_ref[...] += branch(hn_s[...])


def _ffn(h, gain, w_in, w_out, *, tm=1024, tf=512):
    nf = D_FF // tf
    return pl.pallas_call(
        _ffn_kernel,
        grid=(TOKENS // tm, nf),
        in_specs=[pl.BlockSpec((tm, D_MODEL), lambda i, f: (i, 0)),
                  pl.BlockSpec((1, D_MODEL), lambda i, f: (0, 0)),
                  pl.BlockSpec((D_MODEL, tf), lambda i, f: (0, f)),
                  pl.BlockSpec((D_MODEL, tf), lambda i, f: (0, f + nf)),
                  pl.BlockSpec((tf, D_MODEL), lambda i, f: (f, 0))],
        out_specs=pl.BlockSpec((tm, D_MODEL), lambda i, f: (i, 0)),
        out_shape=jax.ShapeDtypeStruct((TOKENS, D_MODEL), F32),
        scratch_shapes=[pltpu.VMEM((tm, D_MODEL), BF16)],
        compiler_params=_params("parallel", "arbitrary"),
        name="swiglu_ffn",
    )(h, gain.reshape(1, D_MODEL), w_in, w_in, w_out)


def _kv_heads_kernel(c_ref, kr_ref, gl_ref, w_ref, gn_ref, gr_ref, rc_ref, rs_ref,
                     k_ref, v_ref, cn_s, krr_s, ssr_s, *, heads):
    @pl.when(pl.program_id(1) == 0)
    def _():
        cn_s[...] = _rms(c_ref[...], gl_ref[...]).astype(BF16)
        kr = kr_ref[...]
        ssr_s[...] = jnp.broadcast_to(jnp.sum(kr * kr, axis=1, keepdims=True), ssr_s.shape)
        krr_s[...] = _rope(kr * gr_ref[...], rc_ref[...], rs_ref[...])

    ones = jnp.ones((v_ref.shape[0], HEAD_PAD - V_D), v_ref.dtype)
    kv_all = jnp.dot(cn_s[...], w_ref[...], preferred_element_type=F32)
    for g in range(heads):
        base = g * HEAD_PAD
        kv = kv_all[:, base:base + HEAD_PAD]
        kn = kv[:, :NOPE_D]
        ss = jnp.sum(kn * kn, axis=1, keepdims=True) + ssr_s[...]
        rstd = lax.rsqrt(ss * (1.0 / QK_D) + NORM_EPS)
        k_ref[:, base:base + NOPE_D] = (kn * rstd * gn_ref[...]).astype(k_ref.dtype)
        k_ref[:, base + NOPE_D:base + HEAD_PAD] = (krr_s[...] * rstd).astype(k_ref.dtype)
        v_ref[:, base:base + V_D] = kv[:, NOPE_D:].astype(v_ref.dtype)
        v_ref[:, base + V_D:base + HEAD_PAD] = ones


def _kv_heads(ckr, gl, w_ukv, gn, gr, rc, rs, *, tm=1024, heads=8):
    rope_blk = KV_LORA // LANES
    row = lambda i, h: (i, 0)
    vec = lambda i, h: (0, 0)
    return pl.pallas_call(
        functools.partial(_kv_heads_kernel, heads=heads),
        grid=(TOKENS // tm, N_HEADS // heads),
        in_specs=[pl.BlockSpec((tm, KV_LORA), row),
                  pl.BlockSpec((tm, LANES), lambda i, h: (i, rope_blk)),
                  pl.BlockSpec((1, KV_LORA), vec),
                  pl.BlockSpec((KV_LORA, heads * HEAD_PAD), lambda i, h: (0, h)),
                  pl.BlockSpec((1, LANES), vec),
                  pl.BlockSpec((1, LANES), vec),
                  pl.BlockSpec((tm, LANES), row),
                  pl.BlockSpec((tm, LANES), row)],
        out_specs=[pl.BlockSpec((tm, heads * HEAD_PAD), lambda i, h: (i, h)),
                   pl.BlockSpec((tm, heads * HEAD_PAD), lambda i, h: (i, h))],
        out_shape=[jax.ShapeDtypeStruct((TOKENS, N_HEADS * HEAD_PAD), BF16),
                   jax.ShapeDtypeStruct((TOKENS, N_HEADS * HEAD_PAD), BF16)],
        scratch_shapes=[pltpu.VMEM((tm, KV_LORA), BF16),
                        pltpu.VMEM((tm, LANES), F32),
                        pltpu.VMEM((tm, LANES), F32)],
        compiler_params=_params("parallel", "arbitrary"),
        name="kv_heads",
    )(ckr, ckr, gl.reshape(1, KV_LORA), w_ukv, gn, gr, rc, rs)


def _q_heads_kernel(c_ref, gl_ref, w_ref, gn_ref, gr_ref, gs_ref, rc_ref, rs_ref,
                    q_ref, cn_s, cg_s, sg_s, *, heads):
    @pl.when(pl.program_id(1) == 0)
    def _():
        cn_s[...] = _rms(c_ref[...], gl_ref[...]).astype(BF16)
        cg_s[...] = rc_ref[...] * gr_ref[...]
        sg_s[...] = rs_ref[...] * gs_ref[...]

    q_all = jnp.dot(cn_s[...], w_ref[...], preferred_element_type=F32)
    for g in range(heads):
        q = q_all[:, g * Q_HEAD_COLS:(g + 1) * Q_HEAD_COLS]
        qn = q[:, :NOPE_D]
        qr = q[:, NOPE_D:NOPE_D + LANES]
        qs = q[:, NOPE_D + LANES:]
        ss = jnp.sum(qn * qn + qr * qr, axis=1, keepdims=True)
        rstd = lax.rsqrt(ss * (1.0 / QK_D) + NORM_EPS)
        base = g * HEAD_PAD
        q_ref[:, base:base + NOPE_D] = (qn * rstd * gn_ref[...]).astype(q_ref.dtype)
        roped = qr * cg_s[...] + qs * sg_s[...]
        q_ref[:, base + NOPE_D:base + HEAD_PAD] = (roped * rstd).astype(q_ref.dtype)


def _q_heads(cq, gl, w_uq3, gn, gr, gs, rc, rs, *, tm=1024, heads=8):
    row = lambda i, h: (i, 0)
    vec = lambda i, h: (0, 0)
    return pl.pallas_call(
        functools.partial(_q_heads_kernel, heads=heads),
        grid=(TOKENS // tm, N_HEADS // heads),
        in_specs=[pl.BlockSpec((tm, Q_LORA), row),
                  pl.BlockSpec((1, Q_LORA), vec),
                  pl.BlockSpec((Q_LORA, heads * Q_HEAD_COLS), lambda i, h: (0, h)),
                  pl.BlockSpec((1, LANES), vec),
                  pl.BlockSpec((1, LANES), vec),
                  pl.BlockSpec((1, LANES), vec),
                  pl.BlockSpec((tm, LANES), row),
                  pl.BlockSpec((tm, LANES), row)],
        out_specs=pl.BlockSpec((tm, heads * HEAD_PAD), lambda i, h: (i, h)),
        out_shape=jax.ShapeDtypeStruct((TOKENS, N_HEADS * HEAD_PAD), BF16),
        scratch_shapes=[pltpu.VMEM((tm, Q_LORA), BF16),
                        pltpu.VMEM((tm, LANES), F32),
                        pltpu.VMEM((tm, LANES), F32)],
        compiler_params=_params("parallel", "arbitrary"),
        name="q_heads",
    )(cq, gl.reshape(1, Q_LORA), w_uq3, gn, gr, gs, rc, rs)


def _attn_kernel(q_ref, k_ref, v_ref, wi_ref, wo_ref, o_ref, wi_b_ref, wo_b_ref, m_s, acc_s, *,
                 blk, heads, subs):
    wi_b_ref[...] = wi_ref[...].astype(wi_b_ref.dtype)
    wo_b_ref[...] = wo_ref[...].astype(wo_b_ref.dtype)
    i = pl.program_id(2)
    m_s[...] = jnp.full_like(m_s, NEG_BIG)
    acc_s[...] = jnp.zeros_like(acc_s)

    def chain(g, row0, nrows, start, nkeys, mask_off):
        cols = slice(g * HEAD_PAD, (g + 1) * HEAD_PAD)
        rows = slice(row0, row0 + nrows)
        k = k_ref[pl.ds(start, nkeys), cols]
        v = v_ref[pl.ds(start, nkeys), cols]
        s = lax.dot_general(q_ref[rows, cols], k, (((1,), (1,)), ((), ())),
                            preferred_element_type=F32)
        if mask_off is not None:
            row = lax.broadcasted_iota(jnp.int32, (nrows, nkeys), 0)
            col = lax.broadcasted_iota(jnp.int32, (nrows, nkeys), 1)
            s = jnp.where(col <= row + mask_off, s, NEG_BIG)
        m_prev = m_s[g, rows, :]
        m_new = jnp.maximum(m_prev, jnp.max(s, axis=1, keepdims=True))
        p = jnp.exp2(s - jnp.concatenate([m_new] * (nkeys // LANES), axis=1))
        alpha = jnp.exp2(m_prev - m_new)
        pv = jnp.dot(p.astype(BF16), v, preferred_element_type=F32)
        acc_s[g, rows, :] = (jnp.concatenate([alpha] * (HEAD_PAD // LANES), axis=1)
                             * acc_s[g, rows, :] + pv)
        m_s[g, rows, :] = m_new

    def step(j, active):
        start = pl.multiple_of(j * blk, blk)
        for g in range(heads):
            for u, masked in active:
                chain(g, u * blk, blk, start, blk, 0 if masked else None)

    def body(j, carry):
        for r in range(KV_UNROLL):
            step(KV_UNROLL * j + r, [(u, False) for u in range(subs)])
        return carry

    assert subs % KV_UNROLL == 0
    lax.fori_loop(0, (subs // KV_UNROLL) * i, body, 0)
    for d in range(subs):
        step(subs * i + d, [(d, True)] + [(u, False) for u in range(d + 1, subs)])
    for g in range(heads):
        acc = acc_s[g]
        o_ref[:, g * V_D:(g + 1) * V_D] = (acc[:, :V_D] / acc[:, V_D:]).astype(o_ref.dtype)


def _attention(q, k, v, ffn_w_in, ffn_w_out, layer, *, blk=512, heads=2, subs=4):
    tq = subs * blk
    nq = SEQ // tq
    nh = N_HEADS // heads
    bands = BATCH * nh
    wi_blk = (D_MODEL // bands, 2 * D_FF // nq)
    wo_blk = (D_FF // bands, D_MODEL // nq)
    tile = lambda b, h, i: (b * nh + h, i)
    src_tile = lambda b, h, i: (layer, b * nh + h, i)
    return pl.pallas_call(
        functools.partial(_attn_kernel, blk=blk, heads=heads, subs=subs),
        grid=(BATCH, nh, nq),
        in_specs=[pl.BlockSpec((tq, heads * HEAD_PAD), lambda b, h, i: (b * nq + i, h)),
                  pl.BlockSpec((SEQ, heads * HEAD_PAD), lambda b, h, i: (b, h)),
                  pl.BlockSpec((SEQ, heads * HEAD_PAD), lambda b, h, i: (b, h)),
                  pl.BlockSpec((None,) + wi_blk, src_tile),
                  pl.BlockSpec((None,) + wo_blk, src_tile)],
        out_specs=[pl.BlockSpec((tq, heads * V_D), lambda b, h, i: (b * nq + i, h)),
                   pl.BlockSpec(wi_blk, tile),
                   pl.BlockSpec(wo_blk, tile)],
        out_shape=[jax.ShapeDtypeStruct((TOKENS, N_HEADS * V_D), BF16),
                   jax.ShapeDtypeStruct((D_MODEL, 2 * D_FF), BF16),
                   jax.ShapeDtypeStruct((D_FF, D_MODEL), BF16)],
        scratch_shapes=[pltpu.VMEM((heads, tq, LANES), F32),
                        pltpu.VMEM((heads, tq, HEAD_PAD), F32)],
        compiler_params=_params("parallel", "parallel", "arbitrary"),
        name="flash_attention",
    )(q, k, v, ffn_w_in, ffn_w_out)


def _swap_halves(t):
    half = ROPE_D // 2
    return jnp.concatenate([t[..., half:], t[..., :half]], axis=-1)


def _split_qk_gain(g, scale):
    rope = lambda v: _spread_rope(v * scale).reshape(1, LANES)
    return (g[:NOPE_D] * scale).reshape(1, LANES), rope(g[NOPE_D:]), rope(_swap_halves(g[NOPE_D:]))


def kernel(x, positions, norm_mix, norm_ffn, lru_w_in, lru_b_in, lru_conv_w, lru_conv_b, lru_w_gate, lru_b_gate, lru_lambda, lru_w_out, lru_b_out, kv_norm_in, w_dkv, kv_latent_norm, w_ukv, k_norm, w_dq, q_latent_norm, w_uq, q_norm, w_o, ffn_w_in, ffn_w_out):
    h = x.reshape(TOKENS, D_MODEL)
    rc, rs = _rope_tables(positions)

    gate, rec_pre, lru_w_out_b, w_o_b = _inproj(h, norm_mix[0], lru_w_in[0].astype(BF16),
                                                lru_b_in[0], lru_w_out, w_o)
    y, ffn_wi0, ffn_wo0 = _rglru(rec_pre, gate, lru_conv_w[0], lru_conv_b[0],
                                 (0.5 * lru_w_gate[0]).astype(BF16), 0.5 * lru_b_gate[0],
                                 lru_lambda[0], ffn_w_in, ffn_w_out, 0)
    h = _linear(y, lru_w_out_b, bias=lru_b_out[0], resid=h, tn=D_MODEL,
                name="lru_out_proj")
    h = _ffn(h, norm_ffn[0], ffn_wi0, ffn_wo0)

    w_dkv_p = jnp.concatenate([w_dkv[:, :KV_LORA], _spread_rope(w_dkv[:, KV_LORA:])],
                              axis=1).astype(BF16)
    ckr, cq = _down_proj(h, kv_norm_in, norm_mix[1], w_dkv_p, w_dq[0].astype(BF16))
    kgn, kgr, _ = _split_qk_gain(k_norm, 1.0)
    k_sh, v_sh = _kv_heads(ckr, kv_latent_norm,
                           w_ukv.reshape(KV_LORA, N_HEADS * (NOPE_D + V_D)).astype(BF16),
                           kgn, kgr, rc, rs)

    w_rope = w_uq[0][..., NOPE_D:]
    w_uq3 = jnp.concatenate([w_uq[0][..., :NOPE_D], _spread_rope(w_rope),
                             _spread_rope(_swap_halves(w_rope))], axis=-1)
    qgn, qgr, qgs = _split_qk_gain(q_norm[0], math.log2(math.e) / math.sqrt(QK_D))
    q = _q_heads(cq, q_latent_norm[0],
                 w_uq3.reshape(Q_LORA, N_HEADS * Q_HEAD_COLS).astype(BF16),
                 qgn, qgr, qgs, rc, rs)
    o, ffn_wi1, ffn_wo1 = _attention(q, k_sh, v_sh, ffn_w_in, ffn_w_out, 1)
    h = _linear(o, w_o_b, resid=h, tn=D_MODEL, name="attn_out_proj")
    h = _ffn(h, norm_ffn[1], ffn_wi1, ffn_wo1)
    return h.reshape(BATCH, SEQ, D_MODEL)
```

```python
import functools
import math

import jax
import jax.numpy as jnp
from jax import lax
from jax.experimental import pallas as pl
from jax.experimental.pallas import tpu as pltpu

F32 = jnp.float32
BF16 = jnp.bfloat16

D_MODEL = 2048
BATCH = 4
SEQ = 4096
TOKENS = BATCH * SEQ
D_RNN = D_MODEL
LRU_BLOCK = 256
N_BLK = D_RNN // LRU_BLOCK
CONV_W = 4
LRU_C = 8.0
N_HEADS = 16
NOPE_D = 128
ROPE_D = 64
QK_D = NOPE_D + ROPE_D
V_D = 128
Q_LORA = 512
KV_LORA = 512
ROPE_THETA = 10000.0
D_FF = 5632
NORM_EPS = 1e-6

LANES = 128
SUBLANES = 8
HEAD_PAD = 2 * LANES
ROPE_T2_LANE = LANES // 2
KV_UNROLL = 4
Q_HEAD_COLS = 3 * LANES
DOWN_PROJ_CHUNKS = 2
LANE_GROUPS = D_RNN // LANES
SCAN_GROUPS = 8
V7X_VMEM_BYTES = 64 * 1024 * 1024
VMEM_LIMIT = V7X_VMEM_BYTES * 7 // 8
NEG_BIG = -1e30


def _params(*semantics):
    return pltpu.CompilerParams(dimension_semantics=semantics, vmem_limit_bytes=VMEM_LIMIT)


def _rms(x, g):
    ms = jnp.mean(x * x, axis=-1, keepdims=True)
    return x * lax.rsqrt(ms + NORM_EPS) * g


def _rope(t, c, s):
    return t * c + pltpu.roll(t, ROPE_T2_LANE, 1) * s


def _spread_rope(v):
    half = ROPE_D // 2
    z = jnp.zeros(v.shape[:-1] + (ROPE_T2_LANE - half,), v.dtype)
    return jnp.concatenate([v[..., :half], z, v[..., half:], z], axis=-1)


def _rope_table_kernel(pos_ref, invf_ref, rc_ref, rs_ref):
    half = ROPE_D // 2
    per_row = LANES // half
    tr = pos_ref.shape[0]
    ang = pos_ref[...].astype(F32) * invf_ref[...]
    cos = jnp.cos(ang)
    sin = jnp.sin(ang)
    low = lax.broadcasted_iota(jnp.int32, (tr, LANES), 1) < half
    for p in range(per_row):
        shift = (LANES - half * p) % LANES
        c = jnp.where(low, pltpu.roll(cos, shift, 1) if shift else cos, 0.0)
        s = jnp.where(low, pltpu.roll(sin, shift, 1) if shift else sin, 0.0)
        rc_ref[pl.ds(p, tr, stride=per_row), :] = c + pltpu.roll(c, ROPE_T2_LANE, 1)
        rs_ref[pl.ds(p, tr, stride=per_row), :] = pltpu.roll(s, ROPE_T2_LANE, 1) - s


def _rope_tables(positions, *, tr=512):
    half = ROPE_D // 2
    per_row = LANES // half
    rows = TOKENS // per_row
    inv_freq = ROPE_THETA ** (-jnp.arange(0, ROPE_D, 2, dtype=F32) / ROPE_D)
    pos = jnp.broadcast_to(positions.reshape(rows, per_row, 1), (rows, per_row, half))
    pos = pos.reshape(rows, LANES)
    invf = jnp.tile(inv_freq, per_row).reshape(1, LANES)
    return pl.pallas_call(
        _rope_table_kernel,
        grid=(rows // tr,),
        in_specs=[pl.BlockSpec((tr, LANES), lambda i: (i, 0)),
                  pl.BlockSpec((1, LANES), lambda i: (0, 0))],
        out_specs=[pl.BlockSpec((per_row * tr, LANES), lambda i: (i, 0)),
                   pl.BlockSpec((per_row * tr, LANES), lambda i: (i, 0))],
        out_shape=[jax.ShapeDtypeStruct((TOKENS, LANES), F32)] * 2,
        compiler_params=_params("arbitrary"),
        name="rope_tables",
    )(pos, invf)


def _linear_kernel(*refs, has_gain, has_bias, has_resid, stage_x):
    refs = list(refs)
    x_ref = refs.pop(0)
    w_ref = refs.pop(0)
    g_ref = refs.pop(0) if has_gain else None
    b_ref = refs.pop(0) if has_bias else None
    r_ref = refs.pop(0) if has_resid else None
    o_ref = refs.pop(0)
    if stage_x:
        xs_ref = refs.pop(0)

        @pl.when(pl.program_id(1) == 0)
        def _():
            x = x_ref[...].astype(F32)
            if has_gain:
                x = _rms(x, g_ref[...])
            xs_ref[...] = x.astype(BF16)

        xv = xs_ref[...]
    else:
        xv = x_ref[...]
    acc = jnp.dot(xv, w_ref[...], preferred_element_type=F32)
    if has_bias:
        acc = acc + b_ref[...]
    if has_resid:
        acc = acc + r_ref[...]
    o_ref[...] = acc.astype(o_ref.dtype)


def _linear(x, w, *, gain=None, bias=None, resid=None, out_dtype=F32, tm=512, tn=1024, name):
    m, k = x.shape
    n = w.shape[1]
    tn = min(tn, n)
    stage_x = gain is not None or x.dtype != BF16
    in_specs = [pl.BlockSpec((tm, k), lambda i, j: (i, 0)),
                pl.BlockSpec((k, tn), lambda i, j: (0, j))]
    args = [x, w]
    if gain is not None:
        in_specs.append(pl.BlockSpec((1, k), lambda i, j: (0, 0)))
        args.append(gain.reshape(1, k))
    if bias is not None:
        in_specs.append(pl.BlockSpec((1, tn), lambda i, j: (0, j)))
        args.append(bias.reshape(1, n))
    if resid is not None:
        in_specs.append(pl.BlockSpec((tm, tn), lambda i, j: (i, j)))
        args.append(resid)
    scratch = [pltpu.VMEM((tm, k), BF16)] if stage_x else []
    return pl.pallas_call(
        functools.partial(_linear_kernel, has_gain=gain is not None, has_bias=bias is not None,
                          has_resid=resid is not None, stage_x=stage_x),
        grid=(m // tm, n // tn),
        in_specs=in_specs,
        out_specs=pl.BlockSpec((tm, tn), lambda i, j: (i, j)),
        out_shape=jax.ShapeDtypeStruct((m, n), out_dtype),
        scratch_shapes=scratch,
        compiler_params=_params("parallel", "arbitrary"),
        name=name,
    )(*args)


def _down_proj_kernel(x_ref, gkv_ref, gq_ref, wkv_ref, wq_ref, ckr_ref, cq_ref):
    chunk = x_ref.shape[0] // DOWN_PROJ_CHUNKS
    for r in range(DOWN_PROJ_CHUNKS):
        rows = slice(r * chunk, (r + 1) * chunk)
        x = x_ref[rows, :]
        xhat = x * lax.rsqrt(jnp.mean(x * x, axis=-1, keepdims=True) + NORM_EPS)
        ckr_ref[rows, :] = jnp.dot((xhat * gkv_ref[...]).astype(BF16), wkv_ref[...],
                                   preferred_element_type=F32)
        cq_ref[rows, :] = jnp.dot((xhat * gq_ref[...]).astype(BF16), wq_ref[...],
                                  preferred_element_type=F32)


def _down_proj(x, g_kv, g_q, w_dkv, w_dq, *, tm=1024):
    n_kv = w_dkv.shape[1]
    row = lambda i: (i, 0)
    full = lambda i: (0, 0)
    return pl.pallas_call(
        _down_proj_kernel,
        grid=(TOKENS // tm,),
        in_specs=[pl.BlockSpec((tm, D_MODEL), row),
                  pl.BlockSpec((1, D_MODEL), full),
                  pl.BlockSpec((1, D_MODEL), full),
                  pl.BlockSpec((D_MODEL, n_kv), full),
                  pl.BlockSpec((D_MODEL, Q_LORA), full)],
        out_specs=[pl.BlockSpec((tm, n_kv), row),
                   pl.BlockSpec((tm, Q_LORA), row)],
        out_shape=[jax.ShapeDtypeStruct((TOKENS, n_kv), F32),
                   jax.ShapeDtypeStruct((TOKENS, Q_LORA), F32)],
        compiler_params=_params("parallel"),
        name="mla_down_proj",
    )(x, g_kv.reshape(1, D_MODEL), g_q.reshape(1, D_MODEL), w_dkv, w_dq)


def _gelu_tanh(x):
    k = -2.0 * math.sqrt(2.0 / math.pi) * math.log2(math.e)
    e = jnp.exp2(x * (k + (k * 0.044715) * (x * x)))
    return x / (1.0 + e)


def _inproj_kernel(x_ref, g_ref, wa_ref, wb_ref, ba_ref, bb_ref, w1_ref, w2_ref,
                   gate_ref, rec_ref, w1_b_ref, w2_b_ref, xs_ref):
    w1_b_ref[...] = w1_ref[...].astype(w1_b_ref.dtype)
    w2_b_ref[...] = w2_ref[...].astype(w2_b_ref.dtype)

    def branch(xv):
        ga = jnp.dot(xv, wa_ref[...], preferred_element_type=F32) + ba_ref[...]
        gate_ref[...] = _gelu_tanh(ga).astype(gate_ref.dtype)
        rec_ref[...] = jnp.dot(xv, wb_ref[...], preferred_element_type=F32) + bb_ref[...]

    @pl.when(pl.program_id(1) == 0)
    def _():
        xv = _rms(x_ref[...], g_ref[...]).astype(BF16)
        xs_ref[...] = xv
        branch(xv)

    @pl.when(pl.program_id(1) != 0)
    def _():
        branch(xs_ref[...])


def _inproj(x, gain, w_in, b_in, sq1, sq2, *, tm=1024, tn=512):
    ni = TOKENS // tm
    nj = D_RNN // tn
    b2 = b_in.reshape(1, 2 * D_RNN)
    sq_blk = (D_MODEL // ni, D_MODEL // nj)
    sq_in = pl.BlockSpec((None,) + sq_blk, lambda i, j: (0, i, j))
    sq_out = pl.BlockSpec(sq_blk, lambda i, j: (i, j))
    sq_shape = jax.ShapeDtypeStruct((D_MODEL, D_MODEL), BF16)
    return pl.pallas_call(
        _inproj_kernel,
        grid=(ni, nj),
        in_specs=[pl.BlockSpec((tm, D_MODEL), lambda i, j: (i, 0)),
                  pl.BlockSpec((1, D_MODEL), lambda i, j: (0, 0)),
                  pl.BlockSpec((D_MODEL, tn), lambda i, j: (0, j)),
                  pl.BlockSpec((D_MODEL, tn), lambda i, j: (0, j + nj)),
                  pl.BlockSpec((1, tn), lambda i, j: (0, j)),
                  pl.BlockSpec((1, tn), lambda i, j: (0, j + nj)),
                  sq_in, sq_in],
        out_specs=[pl.BlockSpec((tm, tn), lambda i, j: (i, j)),
                   pl.BlockSpec((tm, tn), lambda i, j: (i, j)),
                   sq_out, sq_out],
        out_shape=[jax.ShapeDtypeStruct((TOKENS, D_RNN), BF16),
                   jax.ShapeDtypeStruct((TOKENS, D_RNN), F32),
                   sq_shape, sq_shape],
        scratch_shapes=[pltpu.VMEM((tm, D_MODEL), BF16)],
        compiler_params=_params("parallel", "arbitrary"),
        name="lru_in_proj",
    )(x, gain.reshape(1, D_MODEL), w_in, w_in, b2, b2, sq1, sq2)


def _rglru_kernel(x_ref, gate_ref, cw_ref, cb_ref, wg_ref, bg_ref, lam_ref, wi_ref, wo_ref,
                  y_ref, wi_b_ref, wo_b_ref,
                  xp_s, a_s, b_s, tail_s, eq_s, cm_s, hc_s, *, chunk):
    wi_b_ref[...] = wi_ref[...].astype(wi_b_ref.dtype)
    wo_b_ref[...] = wo_ref[...].astype(wo_b_ref.dtype)

    c = pl.program_id(1)
    seg = chunk // SUBLANES
    halo = SUBLANES * (CONV_W - 1)

    @pl.when(c == 0)
    def _():
        tail_s[...] = jnp.zeros_like(tail_s)
        hc_s[...] = jnp.zeros_like(hc_s)

    for g in range(LANE_GROUPS):
        lanes = slice(g * LANES, (g + 1) * LANES)
        for j in range(SUBLANES):
            xp_s[g, pl.ds(halo + j, seg, stride=SUBLANES), :] = x_ref[j * seg:(j + 1) * seg, lanes]

    first_sublane = lax.broadcasted_iota(jnp.int32, (SUBLANES, LANES), 0) == 0
    for g in range(LANE_GROUPS):
        for d in range(1, CONV_W):
            slot = slice(SUBLANES * (CONV_W - 1 - d), SUBLANES * (CONV_W - d))
            cur = xp_s[g, halo + SUBLANES * (seg - d):halo + SUBLANES * (seg - d + 1), :]
            prev = tail_s[g, slot, :]
            xp_s[g, slot, :] = jnp.where(first_sublane, pltpu.roll(prev, 1, 0),
                                         pltpu.roll(cur, 1, 0))
            tail_s[g, slot, :] = cur

    lam = lam_ref[...]
    sp = jnp.maximum(-lam, 0.0) + jnp.log1p(jnp.exp(-jnp.abs(lam)))
    half_c_sp = (0.5 * LRU_C) * sp
    exp2_scale = (-math.log2(math.e)) * half_c_sp
    per_blk = LRU_BLOCK // LANES
    for n in range(N_BLK):
        parts = []
        for sub in range(per_blk):
            g = n * per_blk + sub
            lanes = slice(g * LANES, (g + 1) * LANES)
            acc = cb_ref[:, lanes] + cw_ref[CONV_W - 1:CONV_W, lanes] * xp_s[g, halo:halo + chunk, :]
            for d in range(1, CONV_W):
                off = halo - SUBLANES * d
                acc = acc + cw_ref[CONV_W - 1 - d:CONV_W - d, lanes] * xp_s[g, off:off + chunk, :]
            parts.append(acc)
        xn = jnp.concatenate(parts, axis=1)
        cols = slice(n * LRU_BLOCK, (n + 1) * LRU_BLOCK)
        gh = jnp.dot(xn.astype(BF16), wg_ref[n], preferred_element_type=F32) + bg_ref[n:n + 1, :]
        tr = jnp.tanh(gh[:, :LRU_BLOCK])
        i = 0.5 * jnp.tanh(gh[:, LRU_BLOCK:]) + 0.5
        u = tr + 1.0
        neg_log_a = u * half_c_sp[:, cols]
        a = jnp.exp2(u * exp2_scale[:, cols])
        x1 = jnp.tanh(neg_log_a) * (a * a + 1.0)
        mult = jnp.where(x1 > 0.0, x1 * lax.rsqrt(x1), 0.0)
        b = mult * (i * xn)
        for sub in range(per_blk):
            lanes = slice(sub * LANES, (sub + 1) * LANES)
            a_s[n * per_blk + sub] = a[:, lanes]
            b_s[n * per_blk + sub] = b[:, lanes]

    def tile(ref, g, s):
        return ref[g, pl.ds(pl.multiple_of(s * SUBLANES, SUBLANES), SUBLANES), :]

    for g0 in range(0, LANE_GROUPS, SCAN_GROUPS):
        groups = range(g0, g0 + SCAN_GROUPS)

        def pass1(s, carry):
            hs, ps = carry
            a = [tile(a_s, g, s) for g in groups]
            b = [tile(b_s, g, s) for g in groups]
            return (tuple(ai * hi + bi for ai, hi, bi in zip(a, hs, b)),
                    tuple(ai * pi for ai, pi in zip(a, ps)))

        zero = jnp.zeros((SUBLANES, LANES), F32)
        one = jnp.ones((SUBLANES, LANES), F32)
        e, q = lax.fori_loop(0, seg, pass1, ((zero,) * SCAN_GROUPS, (one,) * SCAN_GROUPS),
                             unroll=8)
        for idx, g in enumerate(groups):
            eq_s[0] = e[idx]
            eq_s[1] = q[idx]
            h = hc_s[g, 0:1, :]
            for j in range(SUBLANES):
                cm_s[g, j:j + 1, :] = h
                h = eq_s[1, j:j + 1, :] * h + eq_s[0, j:j + 1, :]
            hc_s[g, 0:1, :] = h

        def pass2(s, hs):
            out = []
            for g, h in zip(groups, hs):
                h = tile(a_s, g, s) * h + tile(b_s, g, s)
                b_s[g, pl.ds(pl.multiple_of(s * SUBLANES, SUBLANES), SUBLANES), :] = h
                out.append(h)
            return tuple(out)

        lax.fori_loop(0, seg, pass2, tuple(cm_s[g] for g in groups), unroll=8)

    for g in range(LANE_GROUPS):
        lanes = slice(g * LANES, (g + 1) * LANES)
        for j in range(SUBLANES):
            rows = slice(j * seg, (j + 1) * seg)
            h = b_s[g, pl.ds(j, seg, stride=SUBLANES), :]
            y_ref[rows, lanes] = gate_ref[rows, lanes] * h.astype(y_ref.dtype)


def _rglru(rec_pre, gate, conv_w, conv_b, w_gate, b_gate, lam, ffn_w_in, ffn_w_out, layer, *,
           chunk=512):
    nc = SEQ // chunk
    steps = BATCH * nc
    wi_rows = D_MODEL // steps
    wo_rows = D_FF // steps
    row = lambda b, c: (b * nc + c, 0)
    band = lambda b, c: (layer, b * nc + c, 0)
    full2 = lambda b, c: (0, 0)
    return pl.pallas_call(
        functools.partial(_rglru_kernel, chunk=chunk),
        grid=(BATCH, nc),
        in_specs=[pl.BlockSpec((chunk, D_RNN), row),
                  pl.BlockSpec((chunk, D_RNN), row),
                  pl.BlockSpec((CONV_W, D_RNN), full2),
                  pl.BlockSpec((1, D_RNN), full2),
                  pl.BlockSpec((N_BLK, LRU_BLOCK, 2 * LRU_BLOCK), lambda b, c: (0, 0, 0)),
                  pl.BlockSpec((N_BLK, 2 * LRU_BLOCK), full2),
                  pl.BlockSpec((1, D_RNN), full2),
                  pl.BlockSpec((None, wi_rows, 2 * D_FF), band),
                  pl.BlockSpec((None, wo_rows, D_MODEL), band)],
        out_specs=[pl.BlockSpec((chunk, D_RNN), row),
                   pl.BlockSpec((wi_rows, 2 * D_FF), row),
                   pl.BlockSpec((wo_rows, D_MODEL), row)],
        out_shape=[jax.ShapeDtypeStruct((TOKENS, D_RNN), BF16),
                   jax.ShapeDtypeStruct((D_MODEL, 2 * D_FF), BF16),
                   jax.ShapeDtypeStruct((D_FF, D_MODEL), BF16)],
        scratch_shapes=[pltpu.VMEM((LANE_GROUPS, SUBLANES * (CONV_W - 1) + chunk, LANES), F32),
                        pltpu.VMEM((LANE_GROUPS, chunk, LANES), F32),
                        pltpu.VMEM((LANE_GROUPS, chunk, LANES), F32),
                        pltpu.VMEM((LANE_GROUPS, SUBLANES * (CONV_W - 1), LANES), F32),
                        pltpu.VMEM((2, SUBLANES, LANES), F32),
                        pltpu.VMEM((LANE_GROUPS, SUBLANES, LANES), F32),
                        pltpu.VMEM((LANE_GROUPS, SUBLANES, LANES), F32)],
        compiler_params=_params("parallel", "arbitrary"),
        name="rglru_core",
    )(rec_pre, gate, conv_w, conv_b.reshape(1, D_RNN), w_gate, b_gate, lam.reshape(1, D_RNN),
      ffn_w_in, ffn_w_out)


def _ffn_kernel(h_ref, g_ref, wa_ref, wb_ref, w2_ref, o_ref, hn_s):
    f = pl.program_id(1)

    def branch(hn):
        ua = jnp.dot(hn, wa_ref[...], preferred_element_type=F32)
        ub = jnp.dot(hn, wb_ref[...], preferred_element_type=F32)
        act = (jax.nn.silu(ua) * ub).astype(BF16)
        return jnp.dot(act, w2_ref[...], preferred_element_type=F32)

    @pl.when(f == 0)
    def _():
        h = h_ref[...]
        hn = _rms(h, g_ref[...]).astype(BF16)
        hn_s[...] = hn
        o_ref[...] = h + branch(hn)

    @pl.when(f != 0)
    def _():
        o_ref[...] += branch(hn_s[...])


def _ffn(h, gain, w_in, w_out, *, tm=1024, tf=512):
    nf = D_FF // tf
    return pl.pallas_call(
        _ffn_kernel,
        grid=(TOKENS // tm, nf),
        in_specs=[pl.BlockSpec((tm, D_MODEL), lambda i, f: (i, 0)),
                  pl.BlockSpec((1, D_MODEL), lambda i, f: (0, 0)),
                  pl.BlockSpec((D_MODEL, tf), lambda i, f: (0, f)),
                  pl.BlockSpec((D_MODEL, tf), lambda i, f: (0, f + nf)),
                  pl.BlockSpec((tf, D_MODEL), lambda i, f: (f, 0))],
        out_specs=pl.BlockSpec((tm, D_MODEL), lambda i, f: (i, 0)),
        out_shape=jax.ShapeDtypeStruct((TOKENS, D_MODEL), F32),
        scratch_shapes=[pltpu.VMEM((tm, D_MODEL), BF16)],
        compiler_params=_params("parallel", "arbitrary"),
        name="swiglu_ffn",
    )(h, gain.reshape(1, D_MODEL), w_in, w_in, w_out)


def _kv_heads_kernel(c_ref, kr_ref, gl_ref, w_ref, gn_ref, gr_ref, rc_ref, rs_ref,
                     k_ref, v_ref, cn_s, krr_s, ssr_s, *, heads):
    @pl.when(pl.program_id(1) == 0)
    def _():
        cn_s[...] = _rms(c_ref[...], gl_ref[...]).astype(BF16)
        kr = kr_ref[...]
        ssr_s[...] = jnp.broadcast_to(jnp.sum(kr * kr, axis=1, keepdims=True), ssr_s.shape)
        krr_s[...] = _rope(kr * gr_ref[...], rc_ref[...], rs_ref[...])

    ones = jnp.ones((v_ref.shape[0], HEAD_PAD - V_D), v_ref.dtype)
    kv_all = jnp.dot(cn_s[...], w_ref[...], preferred_element_type=F32)
    for g in range(heads):
        base = g * HEAD_PAD
        kv = kv_all[:, base:base + HEAD_PAD]
        kn = kv[:, :NOPE_D]
        ss = jnp.sum(kn * kn, axis=1, keepdims=True) + ssr_s[...]
        rstd = lax.rsqrt(ss * (1.0 / QK_D) + NORM_EPS)
        k_ref[:, base:base + NOPE_D] = (kn * rstd * gn_ref[...]).astype(k_ref.dtype)
        k_ref[:, base + NOPE_D:base + HEAD_PAD] = (krr_s[...] * rstd).astype(k_ref.dtype)
        v_ref[:, base:base + V_D] = kv[:, NOPE_D:].astype(v_ref.dtype)
        v_ref[:, base + V_D:base + HEAD_PAD] = ones


def _kv_heads(ckr, gl, w_ukv, gn, gr, rc, rs, *, tm=1024, heads=8):
    rope_blk = KV_LORA // LANES
    row = lambda i, h: (i, 0)
    vec = lambda i, h: (0, 0)
    return pl.pallas_call(
        functools.partial(_kv_heads_kernel, heads=heads),
        grid=(TOKENS // tm, N_HEADS // heads),
        in_specs=[pl.BlockSpec((tm, KV_LORA), row),
                  pl.BlockSpec((tm, LANES), lambda i, h: (i, rope_blk)),
                  pl.BlockSpec((1, KV_LORA), vec),
                  pl.BlockSpec((KV_LORA, heads * HEAD_PAD), lambda i, h: (0, h)),
                  pl.BlockSpec((1, LANES), vec),
                  pl.BlockSpec((1, LANES), vec),
                  pl.BlockSpec((tm, LANES), row),
                  pl.BlockSpec((tm, LANES), row)],
        out_specs=[pl.BlockSpec((tm, heads * HEAD_PAD), lambda i, h: (i, h)),
                   pl.BlockSpec((tm, heads * HEAD_PAD), lambda i, h: (i, h))],
        out_shape=[jax.ShapeDtypeStruct((TOKENS, N_HEADS * HEAD_PAD), BF16),
                   jax.ShapeDtypeStruct((TOKENS, N_HEADS * HEAD_PAD), BF16)],
        scratch_shapes=[pltpu.VMEM((tm, KV_LORA), BF16),
                        pltpu.VMEM((tm, LANES), F32),
                        pltpu.VMEM((tm, LANES), F32)],
        compiler_params=_params("parallel", "arbitrary"),
        name="kv_heads",
    )(ckr, ckr, gl.reshape(1, KV_LORA), w_ukv, gn, gr, rc, rs)


def _q_heads_kernel(c_ref, gl_ref, w_ref, gn_ref, gr_ref, gs_ref, rc_ref, rs_ref,
                    q_ref, cn_s, cg_s, sg_s, *, heads):
    @pl.when(pl.program_id(1) == 0)
    def _():
        cn_s[...] = _rms(c_ref[...], gl_ref[...]).astype(BF16)
        cg_s[...] = rc_ref[...] * gr_ref[...]
        sg_s[...] = rs_ref[...] * gs_ref[...]

    q_all = jnp.dot(cn_s[...], w_ref[...], preferred_element_type=F32)
    for g in range(heads):
        q = q_all[:, g * Q_HEAD_COLS:(g + 1) * Q_HEAD_COLS]
        qn = q[:, :NOPE_D]
        qr = q[:, NOPE_D:NOPE_D + LANES]
        qs = q[:, NOPE_D + LANES:]
        ss = jnp.sum(qn * qn + qr * qr, axis=1, keepdims=True)
        rstd = lax.rsqrt(ss * (1.0 / QK_D) + NORM_EPS)
        base = g * HEAD_PAD
        q_ref[:, base:base + NOPE_D] = (qn * rstd * gn_ref[...]).astype(q_ref.dtype)
        roped = qr * cg_s[...] + qs * sg_s[...]
        q_ref[:, base + NOPE_D:base + HEAD_PAD] = (roped * rstd).astype(q_ref.dtype)


def _q_heads(cq, gl, w_uq3, gn, gr, gs, rc, rs, *, tm=1024, heads=8):
    row = lambda i, h: (i, 0)
    vec = lambda i, h: (0, 0)
    return pl.pallas_call(
        functools.partial(_q_heads_kernel, heads=heads),
        grid=(TOKENS // tm, N_HEADS // heads),
        in_specs=[pl.BlockSpec((tm, Q_LORA), row),
                  pl.BlockSpec((1, Q_LORA), vec),
                  pl.BlockSpec((Q_LORA, heads * Q_HEAD_COLS), lambda i, h: (0, h)),
                  pl.BlockSpec((1, LANES), vec),
                  pl.BlockSpec((1, LANES), vec),
                  pl.BlockSpec((1, LANES), vec),
                  pl.BlockSpec((tm, LANES), row),
                  pl.BlockSpec((tm, LANES), row)],
        out_specs=pl.BlockSpec((tm, heads * HEAD_PAD), lambda i, h: (i, h)),
        out_shape=jax.ShapeDtypeStruct((TOKENS, N_HEADS * HEAD_PAD), BF16),
        scratch_shapes=[pltpu.VMEM((tm, Q_LORA), BF16),
                        pltpu.VMEM((tm, LANES), F32),
                        pltpu.VMEM((tm, LANES), F32)],
        compiler_params=_params("parallel", "arbitrary"),
        name="q_heads",
    )(cq, gl.reshape(1, Q_LORA), w_uq3, gn, gr, gs, rc, rs)


def _attn_kernel(q_ref, k_ref, v_ref, wi_ref, wo_ref, o_ref, wi_b_ref, wo_b_ref, m_s, acc_s, *,
                 blk, heads, subs):
    wi_b_ref[...] = wi_ref[...].astype(wi_b_ref.dtype)
    wo_b_ref[...] = wo_ref[...].astype(wo_b_ref.dtype)
    i = pl.program_id(2)
    m_s[...] = jnp.full_like(m_s, NEG_BIG)
    acc_s[...] = jnp.zeros_like(acc_s)

    def chain(g, row0, nrows, start, nkeys, mask_off):
        cols = slice(g * HEAD_PAD, (g + 1) * HEAD_PAD)
        rows = slice(row0, row0 + nrows)
        k = k_ref[pl.ds(start, nkeys), cols]
        v = v_ref[pl.ds(start, nkeys), cols]
        s = lax.dot_general(q_ref[rows, cols], k, (((1,), (1,)), ((), ())),
                            preferred_element_type=F32)
        if mask_off is not None:
            row = lax.broadcasted_iota(jnp.int32, (nrows, nkeys), 0)
            col = lax.broadcasted_iota(jnp.int32, (nrows, nkeys), 1)
            s = jnp.where(col <= row + mask_off, s, NEG_BIG)
        m_prev = m_s[g, rows, :]
        m_new = jnp.maximum(m_prev, jnp.max(s, axis=1, keepdims=True))
        p = jnp.exp2(s - jnp.concatenate([m_new] * (nkeys // LANES), axis=1))
        alpha = jnp.exp2(m_prev - m_new)
        pv = jnp.dot(p.astype(BF16), v, preferred_element_type=F32)
        acc_s[g, rows, :] = (jnp.concatenate([alpha] * (HEAD_PAD // LANES), axis=1)
                             * acc_s[g, rows, :] + pv)
        m_s[g, rows, :] = m_new

    def step(j, active):
        start = pl.multiple_of(j * blk, blk)
        for g in range(heads):
            for u, masked in active:
                chain(g, u * blk, blk, start, blk, 0 if masked else None)

    def body(j, carry):
        for r in range(KV_UNROLL):
            step(KV_UNROLL * j + r, [(u, False) for u in range(subs)])
        return carry

    assert subs % KV_UNROLL == 0
    lax.fori_loop(0, (subs // KV_UNROLL) * i, body, 0)
    for d in range(subs):
        step(subs * i + d, [(d, True)] + [(u, False) for u in range(d + 1, subs)])
    for g in range(heads):
        acc = acc_s[g]
        o_ref[:, g * V_D:(g + 1) * V_D] = (acc[:, :V_D] / acc[:, V_D:]).astype(o_ref.dtype)


def _attention(q, k, v, ffn_w_in, ffn_w_out, layer, *, blk=512, heads=2, subs=4):
    tq = subs * blk
    nq = SEQ // tq
    nh = N_HEADS // heads
    bands = BATCH * nh
    wi_blk = (D_MODEL // bands, 2 * D_FF // nq)
    wo_blk = (D_FF // bands, D_MODEL // nq)
    tile = lambda b, h, i: (b * nh + h, i)
    src_tile = lambda b, h, i: (layer, b * nh + h, i)
    return pl.pallas_call(
        functools.partial(_attn_kernel, blk=blk, heads=heads, subs=subs),
        grid=(BATCH, nh, nq),
        in_specs=[pl.BlockSpec((tq, heads * HEAD_PAD), lambda b, h, i: (b * nq + i, h)),
                  pl.BlockSpec((SEQ, heads * HEAD_PAD), lambda b, h, i: (b, h)),
                  pl.BlockSpec((SEQ, heads * HEAD_PAD), lambda b, h, i: (b, h)),
                  pl.BlockSpec((None,) + wi_blk, src_tile),
                  pl.BlockSpec((None,) + wo_blk, src_tile)],
        out_specs=[pl.BlockSpec((tq, heads * V_D), lambda b, h, i: (b * nq + i, h)),
                   pl.BlockSpec(wi_blk, tile),
                   pl.BlockSpec(wo_blk, tile)],
        out_shape=[jax.ShapeDtypeStruct((TOKENS, N_HEADS * V_D), BF16),
                   jax.ShapeDtypeStruct((D_MODEL, 2 * D_FF), BF16),
                   jax.ShapeDtypeStruct((D_FF, D_MODEL), BF16)],
        scratch_shapes=[pltpu.VMEM((heads, tq, LANES), F32),
                        pltpu.VMEM((heads, tq, HEAD_PAD), F32)],
        compiler_params=_params("parallel", "parallel", "arbitrary"),
        name="flash_attention",
    )(q, k, v, ffn_w_in, ffn_w_out)


def _swap_halves(t):
    half = ROPE_D // 2
    return jnp.concatenate([t[..., half:], t[..., :half]], axis=-1)


def _split_qk_gain(g, scale):
    rope = lambda v: _spread_rope(v * scale).reshape(1, LANES)
    return (g[:NOPE_D] * scale).reshape(1, LANES), rope(g[NOPE_D:]), rope(_swap_halves(g[NOPE_D:]))


def kernel(x, positions, norm_mix, norm_ffn, lru_w_in, lru_b_in, lru_conv_w, lru_conv_b, lru_w_gate, lru_b_gate, lru_lambda, lru_w_out, lru_b_out, kv_norm_in, w_dkv, kv_latent_norm, w_ukv, k_norm, w_dq, q_latent_norm, w_uq, q_norm, w_o, ffn_w_in, ffn_w_out):
    h = x.reshape(TOKENS, D_MODEL)
    rc, rs = _rope_tables(positions)

    gate, rec_pre, lru_w_out_b, w_o_b = _inproj(h, norm_mix[0], lru_w_in[0].astype(BF16),
                                                lru_b_in[0], lru_w_out, w_o)
    y, ffn_wi0, ffn_wo0 = _rglru(rec_pre, gate, lru_conv_w[0], lru_conv_b[0],
                                 (0.5 * lru_w_gate[0]).astype(BF16), 0.5 * lru_b_gate[0],
                                 lru_lambda[0], ffn_w_in, ffn_w_out, 0)
    h = _linear(y, lru_w_out_b, bias=lru_b_out[0], resid=h, tn=D_MODEL,
                name="lru_out_proj")
    h = _ffn(h, norm_ffn[0], ffn_wi0, ffn_wo0)

    w_dkv_p = jnp.concatenate([w_dkv[:, :KV_LORA], _spread_rope(w_dkv[:, KV_LORA:])],
                              axis=1).astype(BF16)
    ckr, cq = _down_proj(h, kv_norm_in, norm_mix[1], w_dkv_p, w_dq[0].astype(BF16))
    kgn, kgr, _ = _split_qk_gain(k_norm, 1.0)
    k_sh, v_sh = _kv_heads(ckr, kv_latent_norm,
                           w_ukv.reshape(KV_LORA, N_HEADS * (NOPE_D + V_D)).astype(BF16),
                           kgn, kgr, rc, rs)

    w_rope = w_uq[0][..., NOPE_D:]
    w_uq3 = jnp.concatenate([w_uq[0][..., :NOPE_D], _spread_rope(w_rope),
                             _spread_rope(_swap_halves(w_rope))], axis=-1)
    qgn, qgr, qgs = _split_qk_gain(q_norm[0], math.log2(math.e) / math.sqrt(QK_D))
    q = _q_heads(cq, q_latent_norm[0],
                 w_uq3.reshape(Q_LORA, N_HEADS * Q_HEAD_COLS).astype(BF16),
                 qgn, qgr, qgs, rc, rs)
    o, ffn_wi1, ffn_wo1 = _attention(q, k_sh, v_sh, ffn_w_in, ffn_w_out, 1)
    h = _linear(o, w_o_b, resid=h, tn=D_MODEL, name="attn_out_proj")
    h = _ffn(h, norm_ffn[1], ffn_wi1, ffn_wo1)
    return h.reshape(BATCH, SEQ, D_MODEL)
```

```python
import functools
import math

import jax
import jax.numpy as jnp
from jax import lax
from jax.experimental import pallas as pl
from jax.experimental.pallas import tpu as pltpu

F32 = jnp.float32
BF16 = jnp.bfloat16

D_MODEL = 2048
BATCH = 4
SEQ = 4096
TOKENS = BATCH * SEQ
D_RNN = D_MODEL
LRU_BLOCK = 256
N_BLK = D_RNN // LRU_BLOCK
CONV_W = 4
LRU_C = 8.0
N_HEADS = 16
NOPE_D = 128
ROPE_D = 64
QK_D = NOPE_D + ROPE_D
V_D = 128
Q_LORA = 512
KV_LORA = 512
ROPE_THETA = 10000.0
D_FF = 5632
NORM_EPS = 1e-6

LANES = 128
SUBLANES = 8
HEAD_PAD = 2 * LANES
ROPE_T2_LANE = LANES // 2
KV_UNROLL = 4
Q_HEAD_COLS = 3 * LANES
DOWN_PROJ_CHUNKS = 2
LANE_GROUPS = D_RNN // LANES
SCAN_GROUPS = 8
V7X_VMEM_BYTES = 64 * 1024 * 1024
VMEM_LIMIT = V7X_VMEM_BYTES * 7 // 8
NEG_BIG = -1e30


def _params(*semantics):
    return pltpu.CompilerParams(dimension_semantics=semantics, vmem_limit_bytes=VMEM_LIMIT)


def _rms(x, g):
    ms = jnp.mean(x * x, axis=-1, keepdims=True)
    return x * lax.rsqrt(ms + NORM_EPS) * g


def _rope(t, c, s):
    return t * c + pltpu.roll(t, ROPE_T2_LANE, 1) * s


def _spread_rope(v):
    half = ROPE_D // 2
    z = jnp.zeros(v.shape[:-1] + (ROPE_T2_LANE - half,), v.dtype)
    return jnp.concatenate([v[..., :half], z, v[..., half:], z], axis=-1)


def _rope_table_kernel(pos_ref, invf_ref, rc_ref, rs_ref):
    half = ROPE_D // 2
    per_row = LANES // half
    tr = pos_ref.shape[0]
    ang = pos_ref[...].astype(F32) * invf_ref[...]
    cos = jnp.cos(ang)
    sin = jnp.sin(ang)
    low = lax.broadcasted_iota(jnp.int32, (tr, LANES), 1) < half
    for p in range(per_row):
        shift = (LANES - half * p) % LANES
        c = jnp.where(low, pltpu.roll(cos, shift, 1) if shift else cos, 0.0)
        s = jnp.where(low, pltpu.roll(sin, shift, 1) if shift else sin, 0.0)
        rc_ref[pl.ds(p, tr, stride=per_row), :] = c + pltpu.roll(c, ROPE_T2_LANE, 1)
        rs_ref[pl.ds(p, tr, stride=per_row), :] = pltpu.roll(s, ROPE_T2_LANE, 1) - s


def _rope_tables(positions, *, tr=512):
    half = ROPE_D // 2
    per_row = LANES // half
    rows = TOKENS // per_row
    inv_freq = ROPE_THETA ** (-jnp.arange(0, ROPE_D, 2, dtype=F32) / ROPE_D)
    pos = jnp.broadcast_to(positions.reshape(rows, per_row, 1), (rows, per_row, half))
    pos = pos.reshape(rows, LANES)
    invf = jnp.tile(inv_freq, per_row).reshape(1, LANES)
    return pl.pallas_call(
        _rope_table_kernel,
        grid=(rows // tr,),
        in_specs=[pl.BlockSpec((tr, LANES), lambda i: (i, 0)),
                  pl.BlockSpec((1, LANES), lambda i: (0, 0))],
        out_specs=[pl.BlockSpec((per_row * tr, LANES), lambda i: (i, 0)),
                   pl.BlockSpec((per_row * tr, LANES), lambda i: (i, 0))],
        out_shape=[jax.ShapeDtypeStruct((TOKENS, LANES), F32)] * 2,
        compiler_params=_params("arbitrary"),
        name="rope_tables",
    )(pos, invf)


def _linear_kernel(*refs, has_gain, has_bias, has_resid, stage_x):
    refs = list(refs)
    x_ref = refs.pop(0)
    w_ref = refs.pop(0)
    g_ref = refs.pop(0) if has_gain else None
    b_ref = refs.pop(0) if has_bias else None
    r_ref = refs.pop(0) if has_resid else None
    o_ref = refs.pop(0)
    if stage_x:
        xs_ref = refs.pop(0)

        @pl.when(pl.program_id(1) == 0)
        def _():
            x = x_ref[...].astype(F32)
            if has_gain:
                x = _rms(x, g_ref[...])
            xs_ref[...] = x.astype(BF16)

        xv = xs_ref[...]
    else:
        xv = x_ref[...]
    acc = jnp.dot(xv, w_ref[...], preferred_element_type=F32)
    if has_bias:
        acc = acc + b_ref[...]
    if has_resid:
        acc = acc + r_ref[...]
    o_ref[...] = acc.astype(o_ref.dtype)


def _linear(x, w, *, gain=None, bias=None, resid=None, out_dtype=F32, tm=512, tn=1024, name):
    m, k = x.shape
    n = w.shape[1]
    tn = min(tn, n)
    stage_x = gain is not None or x.dtype != BF16
    in_specs = [pl.BlockSpec((tm, k), lambda i, j: (i, 0)),
                pl.BlockSpec((k, tn), lambda i, j: (0, j))]
    args = [x, w]
    if gain is not None:
        in_specs.append(pl.BlockSpec((1, k), lambda i, j: (0, 0)))
        args.append(gain.reshape(1, k))
    if bias is not None:
        in_specs.append(pl.BlockSpec((1, tn), lambda i, j: (0, j)))
        args.append(bias.reshape(1, n))
    if resid is not None:
        in_specs.append(pl.BlockSpec((tm, tn), lambda i, j: (i, j)))
        args.append(resid)
    scratch = [pltpu.VMEM((tm, k), BF16)] if stage_x else []
    return pl.pallas_call(
        functools.partial(_linear_kernel, has_gain=gain is not None, has_bias=bias is not None,
                          has_resid=resid is not None, stage_x=stage_x),
        grid=(m // tm, n // tn),
        in_specs=in_specs,
        out_specs=pl.BlockSpec((tm, tn), lambda i, j: (i, j)),
        out_shape=jax.ShapeDtypeStruct((m, n), out_dtype),
        scratch_shapes=scratch,
        compiler_params=_params("parallel", "arbitrary"),
        name=name,
    )(*args)


def _down_proj_kernel(x_ref, gkv_ref, gq_ref, wkv_ref, wq_ref, ckr_ref, cq_ref):
    chunk = x_ref.shape[0] // DOWN_PROJ_CHUNKS
    for r in range(DOWN_PROJ_CHUNKS):
        rows = slice(r * chunk, (r + 1) * chunk)
        x = x_ref[rows, :]
        xhat = x * lax.rsqrt(jnp.mean(x * x, axis=-1, keepdims=True) + NORM_EPS)
        ckr_ref[rows, :] = jnp.dot((xhat * gkv_ref[...]).astype(BF16), wkv_ref[...],
                                   preferred_element_type=F32)
        cq_ref[rows, :] = jnp.dot((xhat * gq_ref[...]).astype(BF16), wq_ref[...],
                                  preferred_element_type=F32)


def _down_proj(x, g_kv, g_q, w_dkv, w_dq, *, tm=1024):
    n_kv = w_dkv.shape[1]
    row = lambda i: (i, 0)
    full = lambda i: (0, 0)
    return pl.pallas_call(
        _down_proj_kernel,
        grid=(TOKENS // tm,),
        in_specs=[pl.BlockSpec((tm, D_MODEL), row),
                  pl.BlockSpec((1, D_MODEL), full),
                  pl.BlockSpec((1, D_MODEL), full),
                  pl.BlockSpec((D_MODEL, n_kv), full),
                  pl.BlockSpec((D_MODEL, Q_LORA), full)],
        out_specs=[pl.BlockSpec((tm, n_kv), row),
                   pl.BlockSpec((tm, Q_LORA), row)],
        out_shape=[jax.ShapeDtypeStruct((TOKENS, n_kv), F32),
                   jax.ShapeDtypeStruct((TOKENS, Q_LORA), F32)],
        compiler_params=_params("parallel"),
        name="mla_down_proj",
    )(x, g_kv.reshape(1, D_MODEL), g_q.reshape(1, D_MODEL), w_dkv, w_dq)


def _gelu_tanh(x):
    k = -2.0 * math.sqrt(2.0 / math.pi) * math.log2(math.e)
    e = jnp.exp2(x * (k + (k * 0.044715) * (x * x)))
    return x / (1.0 + e)


def _inproj_kernel(x_ref, g_ref, wa_ref, wb_ref, ba_ref, bb_ref, w1_ref, w2_ref,
                   gate_ref, rec_ref, w1_b_ref, w2_b_ref, xs_ref):
    w1_b_ref[...] = w1_ref[...].astype(w1_b_ref.dtype)
    w2_b_ref[...] = w2_ref[...].astype(w2_b_ref.dtype)

    def branch(xv):
        ga = jnp.dot(xv, wa_ref[...], preferred_element_type=F32) + ba_ref[...]
        gate_ref[...] = _gelu_tanh(ga).astype(gate_ref.dtype)
        rec_ref[...] = jnp.dot(xv, wb_ref[...], preferred_element_type=F32) + bb_ref[...]

    @pl.when(pl.program_id(1) == 0)
    def _():
        xv = _rms(x_ref[...], g_ref[...]).astype(BF16)
        xs_ref[...] = xv
        branch(xv)

    @pl.when(pl.program_id(1) != 0)
    def _():
        branch(xs_ref[...])


def _inproj(x, gain, w_in, b_in, sq1, sq2, *, tm=1024, tn=512):
    ni = TOKENS // tm
    nj = D_RNN // tn
    b2 = b_in.reshape(1, 2 * D_RNN)
    sq_blk = (D_MODEL // ni, D_MODEL // nj)
    sq_in = pl.BlockSpec((None,) + sq_blk, lambda i, j: (0, i, j))
    sq_out = pl.BlockSpec(sq_blk, lambda i, j: (i, j))
    sq_shape = jax.ShapeDtypeStruct((D_MODEL, D_MODEL), BF16)
    return pl.pallas_call(
        _inproj_kernel,
        grid=(ni, nj),
        in_specs=[pl.BlockSpec((tm, D_MODEL), lambda i, j: (i, 0)),
                  pl.BlockSpec((1, D_MODEL), lambda i, j: (0, 0)),
                  pl.BlockSpec((D_MODEL, tn), lambda i, j: (0, j)),
                  pl.BlockSpec((D_MODEL, tn), lambda i, j: (0, j + nj)),
                  pl.BlockSpec((1, tn), lambda i, j: (0, j)),
                  pl.BlockSpec((1, tn), lambda i, j: (0, j + nj)),
                  sq_in, sq_in],
        out_specs=[pl.BlockSpec((tm, tn), lambda i, j: (i, j)),
                   pl.BlockSpec((tm, tn), lambda i, j: (i, j)),
                   sq_out, sq_out],
        out_shape=[jax.ShapeDtypeStruct((TOKENS, D_RNN), BF16),
                   jax.ShapeDtypeStruct((TOKENS, D_RNN), F32),
                   sq_shape, sq_shape],
        scratch_shapes=[pltpu.VMEM((tm, D_MODEL), BF16)],
        compiler_params=_params("parallel", "arbitrary"),
        name="lru_in_proj",
    )(x, gain.reshape(1, D_MODEL), w_in, w_in, b2, b2, sq1, sq2)


def _rglru_kernel(x_ref, gate_ref, cw_ref, cb_ref, wg_ref, bg_ref, lam_ref, wi_ref, wo_ref,
                  y_ref, wi_b_ref, wo_b_ref,
                  xp_s, a_s, b_s, tail_s, eq_s, cm_s, hc_s, *, chunk):
    wi_b_ref[...] = wi_ref[...].astype(wi_b_ref.dtype)
    wo_b_ref[...] = wo_ref[...].astype(wo_b_ref.dtype)

    c = pl.program_id(1)
    seg = chunk // SUBLANES
    halo = SUBLANES * (CONV_W - 1)

    @pl.when(c == 0)
    def _():
        tail_s[...] = jnp.zeros_like(tail_s)
        hc_s[...] = jnp.zeros_like(hc_s)

    for g in range(LANE_GROUPS):
        lanes = slice(g * LANES, (g + 1) * LANES)
        for j in range(SUBLANES):
            xp_s[g, pl.ds(halo + j, seg, stride=SUBLANES), :] = x_ref[j * seg:(j + 1) * seg, lanes]

    first_sublane = lax.broadcasted_iota(jnp.int32, (SUBLANES, LANES), 0) == 0
    for g in range(LANE_GROUPS):
        for d in range(1, CONV_W):
            slot = slice(SUBLANES * (CONV_W - 1 - d), SUBLANES * (CONV_W - d))
            cur = xp_s[g, halo + SUBLANES * (seg - d):halo + SUBLANES * (seg - d + 1), :]
            prev = tail_s[g, slot, :]
            xp_s[g, slot, :] = jnp.where(first_sublane, pltpu.roll(prev, 1, 0),
                                         pltpu.roll(cur, 1, 0))
            tail_s[g, slot, :] = cur

    lam = lam_ref[...]
    sp = jnp.maximum(-lam, 0.0) + jnp.log1p(jnp.exp(-jnp.abs(lam)))
    half_c_sp = (0.5 * LRU_C) * sp
    exp2_scale = (-math.log2(math.e)) * half_c_sp
    per_blk = LRU_BLOCK // LANES
    for n in range(N_BLK):
        parts = []
        for sub in range(per_blk):
            g = n * per_blk + sub
            lanes = slice(g * LANES, (g + 1) * LANES)
            acc = cb_ref[:, lanes] + cw_ref[CONV_W - 1:CONV_W, lanes] * xp_s[g, halo:halo + chunk, :]
            for d in range(1, CONV_W):
                off = halo - SUBLANES * d
                acc = acc + cw_ref[CONV_W - 1 - d:CONV_W - d, lanes] * xp_s[g, off:off + chunk, :]
            parts.append(acc)
        xn = jnp.concatenate(parts, axis=1)
        cols = slice(n * LRU_BLOCK, (n + 1) * LRU_BLOCK)
        gh = jnp.dot(xn.astype(BF16), wg_ref[n], preferred_element_type=F32) + bg_ref[n:n + 1, :]
        tr = jnp.tanh(gh[:, :LRU_BLOCK])
        i = 0.5 * jnp.tanh(gh[:, LRU_BLOCK:]) + 0.5
        u = tr + 1.0
        neg_log_a = u * half_c_sp[:, cols]
        a = jnp.exp2(u * exp2_scale[:, cols])
        x1 = jnp.tanh(neg_log_a) * (a * a + 1.0)
        mult = jnp.where(x1 > 0.0, x1 * lax.rsqrt(x1), 0.0)
        b = mult * (i * xn)
        for sub in range(per_blk):
            lanes = slice(sub * LANES, (sub + 1) * LANES)
            a_s[n * per_blk + sub] = a[:, lanes]
            b_s[n * per_blk + sub] = b[:, lanes]

    def tile(ref, g, s):
        return ref[g, pl.ds(pl.multiple_of(s * SUBLANES, SUBLANES), SUBLANES), :]

    for g0 in range(0, LANE_GROUPS, SCAN_GROUPS):
        groups = range(g0, g0 + SCAN_GROUPS)

        def pass1(s, carry):
            hs, ps = carry
            a = [tile(a_s, g, s) for g in groups]
            b = [tile(b_s, g, s) for g in groups]
            return (tuple(ai * hi + bi for ai, hi, bi in zip(a, hs, b)),
                    tuple(ai * pi for ai, pi in zip(a, ps)))

        zero = jnp.zeros((SUBLANES, LANES), F32)
        one = jnp.ones((SUBLANES, LANES), F32)
        e, q = lax.fori_loop(0, seg, pass1, ((zero,) * SCAN_GROUPS, (one,) * SCAN_GROUPS),
                             unroll=8)
        for idx, g in enumerate(groups):
            eq_s[0] = e[idx]
            eq_s[1] = q[idx]
            h = hc_s[g, 0:1, :]
            for j in range(SUBLANES):
                cm_s[g, j:j + 1, :] = h
                h = eq_s[1, j:j + 1, :] * h + eq_s[0, j:j + 1, :]
            hc_s[g, 0:1, :] = h

        def pass2(s, hs):
            out = []
            for g, h in zip(groups, hs):
                h = tile(a_s, g, s) * h + tile(b_s, g, s)
                b_s[g, pl.ds(pl.multiple_of(s * SUBLANES, SUBLANES), SUBLANES), :] = h
                out.append(h)
            return tuple(out)

        lax.fori_loop(0, seg, pass2, tuple(cm_s[g] for g in groups), unroll=8)

    for g in range(LANE_GROUPS):
        lanes = slice(g * LANES, (g + 1) * LANES)
        for j in range(SUBLANES):
            rows = slice(j * seg, (j + 1) * seg)
            h = b_s[g, pl.ds(j, seg, stride=SUBLANES), :]
            y_ref[rows, lanes] = gate_ref[rows, lanes] * h.astype(y_ref.dtype)


def _rglru(rec_pre, gate, conv_w, conv_b, w_gate, b_gate, lam, ffn_w_in, ffn_w_out, layer, *,
           chunk=512):
    nc = SEQ // chunk
    steps = BATCH * nc
    wi_rows = D_MODEL // steps
    wo_rows = D_FF // steps
    row = lambda b, c: (b * nc + c, 0)
    band = lambda b, c: (layer, b * nc + c, 0)
    full2 = lambda b, c: (0, 0)
    return pl.pallas_call(
        functools.partial(_rglru_kernel, chunk=chunk),
        grid=(BATCH, nc),
        in_specs=[pl.BlockSpec((chunk, D_RNN), row),
                  pl.BlockSpec((chunk, D_RNN), row),
                  pl.BlockSpec((CONV_W, D_RNN), full2),
                  pl.BlockSpec((1, D_RNN), full2),
                  pl.BlockSpec((N_BLK, LRU_BLOCK, 2 * LRU_BLOCK), lambda b, c: (0, 0, 0)),
                  pl.BlockSpec((N_BLK, 2 * LRU_BLOCK), full2),
                  pl.BlockSpec((1, D_RNN), full2),
                  pl.BlockSpec((None, wi_rows, 2 * D_FF), band),
                  pl.BlockSpec((None, wo_rows, D_MODEL), band)],
        out_specs=[pl.BlockSpec((chunk, D_RNN), row),
                   pl.BlockSpec((wi_rows, 2 * D_FF), row),
                   pl.BlockSpec((wo_rows, D_MODEL), row)],
        out_shape=[jax.ShapeDtypeStruct((TOKENS, D_RNN), BF16),
                   jax.ShapeDtypeStruct((D_MODEL, 2 * D_FF), BF16),
                   jax.ShapeDtypeStruct((D_FF, D_MODEL), BF16)],
        scratch_shapes=[pltpu.VMEM((LANE_GROUPS, SUBLANES * (CONV_W - 1) + chunk, LANES), F32),
                        pltpu.VMEM((LANE_GROUPS, chunk, LANES), F32),
                        pltpu.VMEM((LANE_GROUPS, chunk, LANES), F32),
                        pltpu.VMEM((LANE_GROUPS, SUBLANES * (CONV_W - 1), LANES), F32),
                        pltpu.VMEM((2, SUBLANES, LANES), F32),
                        pltpu.VMEM((LANE_GROUPS, SUBLANES, LANES), F32),
                        pltpu.VMEM((LANE_GROUPS, SUBLANES, LANES), F32)],
        compiler_params=_params("parallel", "arbitrary"),
        name="rglru_core",
    )(rec_pre, gate, conv_w, conv_b.reshape(1, D_RNN), w_gate, b_gate, lam.reshape(1, D_RNN),
      ffn_w_in, ffn_w_out)


def _ffn_kernel(h_ref, g_ref, wa_ref, wb_ref, w2_ref, o_ref, hn_s):
    f = pl.program_id(1)

    def branch(hn):
        ua = jnp.dot(hn, wa_ref[...], preferred_element_type=F32)
        ub = jnp.dot(hn, wb_ref[...], preferred_element_type=F32)
        act = (jax.nn.silu(ua) * ub).astype(BF16)
        return jnp.dot(act, w2_ref[...], preferred_element_type=F32)

    @pl.when(f == 0)
    def _():
        h = h_ref[...]
        hn = _rms(h, g_ref[...]).astype(BF16)
        hn_s[...] = hn
        o_ref[...] = h + branch(hn)

    @pl.when(f != 0)
    def _():
        o_ref[...] += branch(hn_s[...])


def _ffn(h, gain, w_in, w_out, *, tm=1024, tf=512):
    nf = D_FF // tf
    return pl.pallas_call(
        _ffn_kernel,
        grid=(TOKENS // tm, nf),
        in_specs=[pl.BlockSpec((tm, D_MODEL), lambda i, f: (i, 0)),
                  pl.BlockSpec((1, D_MODEL), lambda i, f: (0, 0)),
                  pl.BlockSpec((D_MODEL, tf), lambda i, f: (0, f)),
                  pl.BlockSpec((D_MODEL, tf), lambda i, f: (0, f + nf)),
                  pl.BlockSpec((tf, D_MODEL), lambda i, f: (f, 0))],
        out_specs=pl.BlockSpec((tm, D_MODEL), lambda i, f: (i, 0)),
        out_shape=jax.ShapeDtypeStruct((TOKENS, D_MODEL), F32),
        scratch_shapes=[pltpu.VMEM((tm, D_MODEL), BF16)],
        compiler_params=_params("parallel", "arbitrary"),
        name="swiglu_ffn",
    )(h, gain.reshape(1, D_MODEL), w_in, w_in, w_out)


def _kv_heads_kernel(c_ref, kr_ref, gl_ref, w_ref, gn_ref, gr_ref, rc_ref, rs_ref,
                     k_ref, v_ref, cn_s, krr_s, ssr_s, *, heads):
    @pl.when(pl.program_id(1) == 0)
    def _():
        cn_s[...] = _rms(c_ref[...], gl_ref[...]).astype(BF16)
        kr = kr_ref[...]
        ssr_s[...] = jnp.broadcast_to(jnp.sum(kr * kr, axis=1, keepdims=True), ssr_s.shape)
        krr_s[...] = _rope(kr * gr_ref[...], rc_ref[...], rs_ref[...])

    kv_all = jnp.dot(cn_s[...], w_ref[...], preferred_element_type=F32)
    for g in range(heads):
        base = g * HEAD_PAD
        kv = kv_all[:, base:base + HEAD_PAD]
        kn = kv[:, :NOPE_D]
        ss = jnp.sum(kn * kn, axis=1, keepdims=True) + ssr_s[...]
        rstd = lax.rsqrt(ss * (1.0 / QK_D) + NORM_EPS)
        k_ref[:, base:base + NOPE_D] = (kn * rstd * gn_ref[...]).astype(k_ref.dtype)
        k_ref[:, base + NOPE_D:base + HEAD_PAD] = (krr_s[...] * rstd).astype(k_ref.dtype)
        v_ref[:, g * V_D:(g + 1) * V_D] = kv[:, NOPE_D:].astype(v_ref.dtype)


def _kv_heads(ckr, gl, w_ukv, gn, gr, rc, rs, *, tm=1024, heads=8):
    rope_blk = KV_LORA // LANES
    row = lambda i, h: (i, 0)
    vec = lambda i, h: (0, 0)
    return pl.pallas_call(
        functools.partial(_kv_heads_kernel, heads=heads),
        grid=(TOKENS // tm, N_HEADS // heads),
        in_specs=[pl.BlockSpec((tm, KV_LORA), row),
                  pl.BlockSpec((tm, LANES), lambda i, h: (i, rope_blk)),
                  pl.BlockSpec((1, KV_LORA), vec),
                  pl.BlockSpec((KV_LORA, heads * HEAD_PAD), lambda i, h: (0, h)),
                  pl.BlockSpec((1, LANES), vec),
                  pl.BlockSpec((1, LANES), vec),
                  pl.BlockSpec((tm, LANES), row),
                  pl.BlockSpec((tm, LANES), row)],
        out_specs=[pl.BlockSpec((tm, heads * HEAD_PAD), lambda i, h: (i, h)),
                   pl.BlockSpec((tm, heads * V_D), lambda i, h: (i, h))],
        out_shape=[jax.ShapeDtypeStruct((TOKENS, N_HEADS * HEAD_PAD), BF16),
                   jax.ShapeDtypeStruct((TOKENS, N_HEADS * V_D), BF16)],
        scratch_shapes=[pltpu.VMEM((tm, KV_LORA), BF16),
                        pltpu.VMEM((tm, LANES), F32),
                        pltpu.VMEM((tm, LANES), F32)],
        compiler_params=_params("parallel", "arbitrary"),
        name="kv_heads",
    )(ckr, ckr, gl.reshape(1, KV_LORA), w_ukv, gn, gr, rc, rs)


def _q_heads_kernel(c_ref, gl_ref, w_ref, gn_ref, gr_ref, gs_ref, rc_ref, rs_ref,
                    q_ref, cn_s, cg_s, sg_s, *, heads):
    @pl.when(pl.program_id(1) == 0)
    def _():
        cn_s[...] = _rms(c_ref[...], gl_ref[...]).astype(BF16)
        cg_s[...] = rc_ref[...] * gr_ref[...]
        sg_s[...] = rs_ref[...] * gs_ref[...]

    q_all = jnp.dot(cn_s[...], w_ref[...], preferred_element_type=F32)
    for g in range(heads):
        q = q_all[:, g * Q_HEAD_COLS:(g + 1) * Q_HEAD_COLS]
        qn = q[:, :NOPE_D]
        qr = q[:, NOPE_D:NOPE_D + LANES]
        qs = q[:, NOPE_D + LANES:]
        ss = jnp.sum(qn * qn + qr * qr, axis=1, keepdims=True)
        rstd = lax.rsqrt(ss * (1.0 / QK_D) + NORM_EPS)
        base = g * HEAD_PAD
        q_ref[:, base:base + NOPE_D] = (qn * rstd * gn_ref[...]).astype(q_ref.dtype)
        roped = qr * cg_s[...] + qs * sg_s[...]
        q_ref[:, base + NOPE_D:base + HEAD_PAD] = (roped * rstd).astype(q_ref.dtype)


def _q_heads(cq, gl, w_uq3, gn, gr, gs, rc, rs, *, tm=1024, heads=8):
    row = lambda i, h: (i, 0)
    vec = lambda i, h: (0, 0)
    return pl.pallas_call(
        functools.partial(_q_heads_kernel, heads=heads),
        grid=(TOKENS // tm, N_HEADS // heads),
        in_specs=[pl.BlockSpec((tm, Q_LORA), row),
                  pl.BlockSpec((1, Q_LORA), vec),
                  pl.BlockSpec((Q_LORA, heads * Q_HEAD_COLS), lambda i, h: (0, h)),
                  pl.BlockSpec((1, LANES), vec),
                  pl.BlockSpec((1, LANES), vec),
                  pl.BlockSpec((1, LANES), vec),
                  pl.BlockSpec((tm, LANES), row),
                  pl.BlockSpec((tm, LANES), row)],
        out_specs=pl.BlockSpec((tm, heads * HEAD_PAD), lambda i, h: (i, h)),
        out_shape=jax.ShapeDtypeStruct((TOKENS, N_HEADS * HEAD_PAD), BF16),
        scratch_shapes=[pltpu.VMEM((tm, Q_LORA), BF16),
                        pltpu.VMEM((tm, LANES), F32),
                        pltpu.VMEM((tm, LANES), F32)],
        compiler_params=_params("parallel", "arbitrary"),
        name="q_heads",
    )(cq, gl.reshape(1, Q_LORA), w_uq3, gn, gr, gs, rc, rs)


def _attn_kernel(q_ref, k_ref, v_ref, wi_ref, wo_ref, o_ref, wi_b_ref, wo_b_ref, m_s, acc_s, *,
                 blk, heads, subs):
    wi_b_ref[...] = wi_ref[...].astype(wi_b_ref.dtype)
    wo_b_ref[...] = wo_ref[...].astype(wo_b_ref.dtype)
    i = pl.program_id(2)
    m_s[...] = jnp.full_like(m_s, NEG_BIG)
    acc_s[...] = jnp.zeros_like(acc_s)

    def chain(g, row0, nrows, start, nkeys, mask_off):
        cols = slice(g * HEAD_PAD, (g + 1) * HEAD_PAD)
        rows = slice(row0, row0 + nrows)
        k = k_ref[pl.ds(start, nkeys), cols]
        v = jnp.concatenate([v_ref[pl.ds(start, nkeys), g * V_D:(g + 1) * V_D],
                             jnp.ones((nkeys, HEAD_PAD - V_D), v_ref.dtype)], axis=1)
        s = lax.dot_general(q_ref[rows, cols], k, (((1,), (1,)), ((), ())),
                            preferred_element_type=F32)
        if mask_off is not None:
            row = lax.broadcasted_iota(jnp.int32, (nrows, nkeys), 0)
            col = lax.broadcasted_iota(jnp.int32, (nrows, nkeys), 1)
            s = jnp.where(col <= row + mask_off, s, NEG_BIG)
        m_prev = m_s[g, rows, :]
        m_new = jnp.maximum(m_prev, jnp.max(s, axis=1, keepdims=True))
        p = jnp.exp2(s - jnp.concatenate([m_new] * (nkeys // LANES), axis=1))
        alpha = jnp.exp2(m_prev - m_new)
        pv = jnp.dot(p.astype(BF16), v, preferred_element_type=F32)
        acc_s[g, rows, :] = (jnp.concatenate([alpha] * (HEAD_PAD // LANES), axis=1)
                             * acc_s[g, rows, :] + pv)
        m_s[g, rows, :] = m_new

    def step(j, active):
        start = pl.multiple_of(j * blk, blk)
        for g in range(heads):
            for u, masked in active:
                chain(g, u * blk, blk, start, blk, 0 if masked else None)

    def body(j, carry):
        for r in range(KV_UNROLL):
            step(KV_UNROLL * j + r, [(u, False) for u in range(subs)])
        return carry

    assert subs % KV_UNROLL == 0
    lax.fori_loop(0, (subs // KV_UNROLL) * i, body, 0)
    for d in range(subs):
        step(subs * i + d, [(d, True)] + [(u, False) for u in range(d + 1, subs)])
    for g in range(heads):
        acc = acc_s[g]
        o_ref[:, g * V_D:(g + 1) * V_D] = (acc[:, :V_D] / acc[:, V_D:]).astype(o_ref.dtype)


def _attention(q, k, v, ffn_w_in, ffn_w_out, layer, *, blk=512, heads=2, subs=4):
    tq = subs * blk
    nq = SEQ // tq
    nh = N_HEADS // heads
    bands = BATCH * nh
    wi_blk = (D_MODEL // bands, 2 * D_FF // nq)
    wo_blk = (D_FF // bands, D_MODEL // nq)
    tile = lambda b, h, i: (b * nh + h, i)
    src_tile = lambda b, h, i: (layer, b * nh + h, i)
    return pl.pallas_call(
        functools.partial(_attn_kernel, blk=blk, heads=heads, subs=subs),
        grid=(BATCH, nh, nq),
        in_specs=[pl.BlockSpec((tq, heads * HEAD_PAD), lambda b, h, i: (b * nq + i, h)),
                  pl.BlockSpec((SEQ, heads * HEAD_PAD), lambda b, h, i: (b, h)),
                  pl.BlockSpec((SEQ, heads * V_D), lambda b, h, i: (b, h)),
                  pl.BlockSpec((None,) + wi_blk, src_tile),
                  pl.BlockSpec((None,) + wo_blk, src_tile)],
        out_specs=[pl.BlockSpec((tq, heads * V_D), lambda b, h, i: (b * nq + i, h)),
                   pl.BlockSpec(wi_blk, tile),
                   pl.BlockSpec(wo_blk, tile)],
        out_shape=[jax.ShapeDtypeStruct((TOKENS, N_HEADS * V_D), BF16),
                   jax.ShapeDtypeStruct((D_MODEL, 2 * D_FF), BF16),
                   jax.ShapeDtypeStruct((D_FF, D_MODEL), BF16)],
        scratch_shapes=[pltpu.VMEM((heads, tq, LANES), F32),
                        pltpu.VMEM((heads, tq, HEAD_PAD), F32)],
        compiler_params=_params("parallel", "parallel", "arbitrary"),
        name="flash_attention",
    )(q, k, v, ffn_w_in, ffn_w_out)


def _swap_halves(t):
    half = ROPE_D // 2
    return jnp.concatenate([t[..., half:], t[..., :half]], axis=-1)


def _split_qk_gain(g, scale):
    rope = lambda v: _spread_rope(v * scale).reshape(1, LANES)
    return (g[:NOPE_D] * scale).reshape(1, LANES), rope(g[NOPE_D:]), rope(_swap_halves(g[NOPE_D:]))


def kernel(x, positions, norm_mix, norm_ffn, lru_w_in, lru_b_in, lru_conv_w, lru_conv_b, lru_w_gate, lru_b_gate, lru_lambda, lru_w_out, lru_b_out, kv_norm_in, w_dkv, kv_latent_norm, w_ukv, k_norm, w_dq, q_latent_norm, w_uq, q_norm, w_o, ffn_w_in, ffn_w_out):
    h = x.reshape(TOKENS, D_MODEL)
    rc, rs = _rope_tables(positions)

    gate, rec_pre, lru_w_out_b, w_o_b = _inproj(h, norm_mix[0], lru_w_in[0].astype(BF16),
                                                lru_b_in[0], lru_w_out, w_o)
    y, ffn_wi0, ffn_wo0 = _rglru(rec_pre, gate, lru_conv_w[0], lru_conv_b[0],
                                 (0.5 * lru_w_gate[0]).astype(BF16), 0.5 * lru_b_gate[0],
                                 lru_lambda[0], ffn_w_in, ffn_w_out, 0)
    h = _linear(y, lru_w_out_b, bias=lru_b_out[0], resid=h, tn=D_MODEL,
                name="lru_out_proj")
    h = _ffn(h, norm_ffn[0], ffn_wi0, ffn_wo0)

    w_dkv_p = jnp.concatenate([w_dkv[:, :KV_LORA], _spread_rope(w_dkv[:, KV_LORA:])],
                              axis=1).astype(BF16)
    ckr, cq = _down_proj(h, kv_norm_in, norm_mix[1], w_dkv_p, w_dq[0].astype(BF16))
    kgn, kgr, _ = _split_qk_gain(k_norm, 1.0)
    k_sh, v_sh = _kv_heads(ckr, kv_latent_norm,
                           w_ukv.reshape(KV_LORA, N_HEADS * (NOPE_D + V_D)).astype(BF16),
                           kgn, kgr, rc, rs)

    w_rope = w_uq[0][..., NOPE_D:]
    w_uq3 = jnp.concatenate([w_uq[0][..., :NOPE_D], _spread_rope(w_rope),
                             _spread_rope(_swap_halves(w_rope))], axis=-1)
    qgn, qgr, qgs = _split_qk_gain(q_norm[0], math.log2(math.e) / math.sqrt(QK_D))
    q = _q_heads(cq, q_latent_norm[0],
                 w_uq3.reshape(Q_LORA, N_HEADS * Q_HEAD_COLS).astype(BF16),
                 qgn, qgr, qgs, rc, rs)
    o, ffn_wi1, ffn_wo1 = _attention(q, k_sh, v_sh, ffn_w_in, ffn_w_out, 1)
    h = _linear(o, w_o_b, resid=h, tn=D_MODEL, name="attn_out_proj")
    h = _ffn(h, norm_ffn[1], ffn_wi1, ffn_wo1)
    return h.reshape(BATCH, SEQ, D_MODEL)
```
